```python
import math
import jax, jax.numpy as jnp
from jax import lax
import numpy as np

D_MODEL = 1024
BATCH = 8
SEQ = 4096
DEPTH = 2
DEC_BATCH = 2
DEC_SEQ = 16384
PAST_LEN = 128

ATT_HEADS = 4
ATT_HEAD_DIM = 64
ATT_V_DIM = 2 * ATT_HEAD_DIM
ATT_QK_WIDTH = ATT_HEADS * 2 * ATT_HEAD_DIM
ATT_WIDTH = ATT_HEADS * ATT_V_DIM
HY_WIDTH = D_MODEL - ATT_WIDTH
HY_ORDER = 2
SHORT_CONV = 3
FILTER_EMB = 33
FILTER_BANDS = (FILTER_EMB - 1) // 2
FILTER_HIDDEN = 64
DECAY_TARGET = 1e-2
FAST_DECAY_PCT = 0.3
SLOW_DECAY_PCT = 1.5
FILTER_SHIFT = 0.05
PROJ_WIDTH = 2 * ATT_QK_WIDTH + ATT_WIDTH + (HY_ORDER + 1) * HY_WIDTH
N_GROUPS = 4
EXPERTS_PER_GROUP = 4
N_EXPERTS = N_GROUPS * EXPERTS_PER_GROUP
TOP_K = 2
D_EXPERT = 512
Q_BLOCK = 128
LN_EPS = 1e-5
RMS_EPS = 1e-5

kernel_name = "hymba_diffattn_hyena_hmoe_encoder"


def layer_norm(x, g, b):
    xf = x.astype(jnp.float32)
    mu = jnp.mean(xf, axis=-1, keepdims=True)
    var = jnp.mean(jnp.square(xf - mu), axis=-1, keepdims=True)
    return ((xf - mu) * lax.rsqrt(var + LN_EPS) * g + b).astype(x.dtype)


def alibi_slopes():
    return jnp.asarray(2.0 ** (-8.0 * np.arange(1, ATT_HEADS + 1) / ATT_HEADS), dtype=jnp.float32)


def diff_attention(q, k, v, lam, lam_init, subln_g):
    B, L = q.shape[0], q.shape[1]
    nb = L // Q_BLOCK
    slopes = alibi_slopes()
    kpos = jnp.arange(L)
    qb = (q * (ATT_HEAD_DIM ** -0.5)).reshape(B, nb, Q_BLOCK, ATT_HEADS, 2, ATT_HEAD_DIM)
    qb = qb.transpose(1, 0, 2, 3, 4, 5)

    def block(args):
        i, qi = args
        qpos = i * Q_BLOCK + jnp.arange(Q_BLOCK)
        dist = jnp.abs(qpos[:, None] - kpos[None, :]).astype(jnp.float32)
        bias = -slopes[:, None, None] * dist[None]
        s = jnp.einsum('bqhcd,bkhcd->bchqk', qi, k).astype(jnp.float32) + bias[None, None]
        p = jax.nn.softmax(s, axis=-1)
        pd = (p[:, 0] - lam * p[:, 1]).astype(v.dtype)
        return jnp.einsum('bhqk,bkhe->bqhe', pd, v)

    o = lax.map(block, (jnp.arange(nb), qb))
    o = o.transpose(1, 0, 2, 3, 4).reshape(B, L, ATT_HEADS, ATT_V_DIM)
    of = o.astype(jnp.float32)
    of = of * lax.rsqrt(jnp.mean(jnp.square(of), axis=-1, keepdims=True) + RMS_EPS)
    of = of * subln_g * (1.0 - lam_init)
    return of.reshape(B, L, ATT_WIDTH).astype(v.dtype)


def short_conv(x, w, b):
    xp = jnp.pad(x, ((0, 0), (1, 1), (0, 0)))
    return xp[:, :-2] * w[0] + xp[:, 1:-1] * w[1] + xp[:, 2:] * w[2] + b


def hyena_filters(L, w1, b1, f1, w2, b2, f2, w3):
    t = jnp.linspace(0.0, 1.0, L, dtype=jnp.float32)[:, None]
    w = 2.0 * math.pi * jnp.arange(L, dtype=jnp.float32) / L
    f = jnp.linspace(1e-4, FILTER_BANDS - 1, FILTER_BANDS, dtype=jnp.float32)
    ang = w[:, None] * f[None, :]
    z = jnp.concatenate([t, jnp.cos(ang), -jnp.sin(ang)], axis=-1)
    h = jnp.sin(f1 * (z @ w1 + b1))
    h = jnp.sin(f2 * (h @ w2 + b2))
    h = (h @ w3).astype(jnp.float32)
    max_decay = abs(math.log(DECAY_TARGET) / FAST_DECAY_PCT)
    min_decay = abs(math.log(DECAY_TARGET) / SLOW_DECAY_PCT)
    deltas = jnp.linspace(min_decay, max_decay, HY_WIDTH, dtype=jnp.float32)
    decay = jnp.exp(-t * deltas[None, :]) + FILTER_SHIFT
    h = h.reshape(L, 2, HY_WIDTH) * decay[:, None, :]
    return h[:, 0], h[:, 1]


def bidir_fftconv(u, h_fwd, h_bwd, d_skip):
    B, L, C = u.shape
    k = jnp.concatenate([h_fwd, jnp.zeros((1, C), jnp.float32), h_bwd[:0:-1]], axis=0)
    kf = jnp.fft.rfft(k, axis=0)
    uf32 = u.astype(jnp.float32)
    uf = jnp.fft.rfft(uf32, n=2 * L, axis=1)
    y = jnp.fft.irfft(uf * kf[None], n=2 * L, axis=1)[:, :L]
    return (y + uf32 * d_skip.astype(jnp.float32)).astype(u.dtype)


def mixer(x, layer_idx, w_in, b_in, conv_w, conv_b, lam_q1, lam_k1, lam_q2, lam_k2, subln_g,
          filt_w1, filt_b1, filt_freq1, filt_w2, filt_b2, filt_freq2, filt_w3, hyena_d, w_out, b_out):
    B, L, _ = x.shape
    p = x @ w_in + b_in
    a0, a1, a2 = ATT_QK_WIDTH, 2 * ATT_QK_WIDTH, 2 * ATT_QK_WIDTH + ATT_WIDTH
    q = p[..., :a0].reshape(B, L, ATT_HEADS, 2, ATT_HEAD_DIM)
    k = p[..., a0:a1].reshape(B, L, ATT_HEADS, 2, ATT_HEAD_DIM)
    v = p[..., a1:a2].reshape(B, L, ATT_HEADS, ATT_V_DIM)
    hy = short_conv(p[..., a2:], conv_w, conv_b)
    x0 = hy[..., :HY_WIDTH]
    x1 = hy[..., HY_WIDTH:2 * HY_WIDTH]
    hv = hy[..., 2 * HY_WIDTH:]
    lam_init = 0.8 - 0.6 * math.exp(-0.3 * layer_idx)
    lam = (jnp.exp(jnp.sum(lam_q1.astype(jnp.float32) * lam_k1.astype(jnp.float32)))
           - jnp.exp(jnp.sum(lam_q2.astype(jnp.float32) * lam_k2.astype(jnp.float32))) + lam_init)
    att = diff_attention(q, k, v, lam, lam_init, subln_g)
    h_fwd, h_bwd = hyena_filters(L, filt_w1, filt_b1, filt_freq1, filt_w2, filt_b2, filt_freq2, filt_w3)
    hyo = x0 * bidir_fftconv(hv * x1, h_fwd, h_bwd, hyena_d)
    return jnp.concatenate([att, hyo], axis=-1) @ w_out + b_out


def hier_moe(x, router_group_w, router_group_b, router_expert_w, router_expert_b, exp_w1, exp_w3, exp_w2):
    B, L, D = x.shape
    T = B * L
    xt = x.reshape(T, D)
    g_logits = (xt @ router_group_w + router_group_b).astype(jnp.float32)
    g_prob = jax.nn.softmax(g_logits, axis=-1)
    g_w, g_idx = lax.top_k(g_prob, 1)
    e_logits = (xt @ router_expert_w + router_expert_b).astype(jnp.float32).reshape(T, N_GROUPS, EXPERTS_PER_GROUP)
    e_in = jnp.take_along_axis(e_logits, g_idx[:, :, None], axis=1)[:, 0]
    e_prob = jax.nn.softmax(e_in, axis=-1)
    top_p, top_i = lax.top_k(e_prob, TOP_K)
    top_p = top_p / jnp.sum(top_p, axis=-1, keepdims=True)
    within = jnp.sum(jax.nn.one_hot(top_i, EXPERTS_PER_GROUP, dtype=jnp.float32) * top_p[..., None], axis=1)
    combine = (jax.nn.one_hot(g_idx[:, 0], N_GROUPS, dtype=jnp.float32)[:, :, None]
               * (g_w * within)[:, None, :]).reshape(T, N_EXPERTS)
    y = jnp.zeros((T, D), jnp.float32)
    for e in range(N_EXPERTS):
        h = jax.nn.silu(xt @ exp_w1[e]) * (xt @ exp_w3[e])
        y = y + combine[:, e:e + 1] * (h @ exp_w2[e]).astype(jnp.float32)
    return y.astype(x.dtype).reshape(B, L, D)


def encoder_layer(x, layer_idx, w_in, b_in, conv_w, conv_b, lam_q1, lam_k1, lam_q2, lam_k2, subln_g,
                  filt_w1, filt_b1, filt_freq1, filt_w2, filt_b2, filt_freq2, filt_w3, hyena_d,
                  w_out, b_out, ln1_g, ln1_b, router_group_w, router_group_b, router_expert_w,
                  router_expert_b, exp_w1, exp_w3, exp_w2, ln2_g, ln2_b):
    alpha = (2.0 * DEPTH) ** 0.25
    m = mixer(x, layer_idx, w_in, b_in, conv_w, conv_b, lam_q1, lam_k1, lam_q2, lam_k2, subln_g,
              filt_w1, filt_b1, filt_freq1, filt_w2, filt_b2, filt_freq2, filt_w3, hyena_d, w_out, b_out)
    x = layer_norm(alpha * x + m, ln1_g, ln1_b)
    f = hier_moe(x, router_group_w, router_group_b, router_expert_w, router_expert_b, exp_w1, exp_w3, exp_w2)
    return layer_norm(alpha * x + f, ln2_g, ln2_b)


def trunk(x, params):
    for l in range(DEPTH):
        lp = [p[l] for p in params]
        x = encoder_layer(x, l, *lp)
    return x


def setup_inputs(seed: int = 0) -> dict:
    key = jax.random.key(seed)
    ks = jax.random.split(key, 32)
    f32 = jnp.float32
    beta = (8.0 * DEPTH) ** -0.25
    nrm = lambda k, shape, s: jax.random.normal(k, shape, f32) * s
    x_prompt = nrm(ks[0], (BATCH, SEQ, D_MODEL), 1.0)
    x_sample = nrm(ks[1], (DEC_BATCH, DEC_SEQ, D_MODEL), 1.0)
    w_in = nrm(ks[2], (DEPTH, D_MODEL, PROJ_WIDTH), D_MODEL ** -0.5)
    v_lo, v_hi = 2 * ATT_QK_WIDTH, 2 * ATT_QK_WIDTH + ATT_WIDTH
    w_in = w_in.at[:, :, v_lo:v_hi].multiply(beta)
    b_in = nrm(ks[3], (DEPTH, PROJ_WIDTH), 0.01)
    conv_w = nrm(ks[4], (DEPTH, SHORT_CONV, (HY_ORDER + 1) * HY_WIDTH), SHORT_CONV ** -0.5)
    conv_b = nrm(ks[5], (DEPTH, (HY_ORDER + 1) * HY_WIDTH), 0.01)
    lam_q1 = nrm(ks[6], (DEPTH, ATT_HEAD_DIM), 0.1)
    lam_k1 = nrm(ks[7], (DEPTH, ATT_HEAD_DIM), 0.1)
    lam_q2 = nrm(ks[8], (DEPTH, ATT_HEAD_DIM), 0.1)
    lam_k2 = nrm(ks[9], (DEPTH, ATT_HEAD_DIM), 0.1)
    subln_g = 1.0 + nrm(ks[10], (DEPTH, ATT_V_DIM), 0.01)
    filt_w1 = nrm(ks[11], (DEPTH, FILTER_EMB, FILTER_HIDDEN), FILTER_EMB ** -0.5)
    filt_b1 = nrm(ks[12], (DEPTH, FILTER_HIDDEN), 0.01)
    filt_freq1 = 1.0 + nrm(ks[13], (DEPTH, FILTER_HIDDEN), 0.01)
    filt_w2 = nrm(ks[14], (DEPTH, FILTER_HIDDEN, FILTER_HIDDEN), FILTER_HIDDEN ** -0.5)
    filt_b2 = nrm(ks[15], (DEPTH, FILTER_HIDDEN), 0.01)
    filt_freq2 = 1.0 + nrm(ks[16], (DEPTH, FILTER_HIDDEN), 0.01)
    filt_w3 = nrm(ks[17], (DEPTH, FILTER_HIDDEN, 2 * HY_WIDTH), 0.02)
    hyena_d = nrm(ks[18], (DEPTH, HY_WIDTH), 1.0)
    w_out = nrm(ks[19], (DEPTH, D_MODEL, D_MODEL), D_MODEL ** -0.5 * beta)
    b_out = nrm(ks[20], (DEPTH, D_MODEL), 0.01)
    ln1_g = 1.0 + nrm(ks[21], (DEPTH, D_MODEL), 0.01)
    ln1_b = nrm(ks[22], (DEPTH, D_MODEL), 0.01)
    router_group_w = nrm(ks[23], (DEPTH, D_MODEL, N_GROUPS), D_MODEL ** -0.5)
    router_group_b = nrm(ks[24], (DEPTH, N_GROUPS), 0.01)
    router_expert_w = nrm(ks[25], (DEPTH, D_MODEL, N_EXPERTS), D_MODEL ** -0.5)
    router_expert_b = nrm(ks[26], (DEPTH, N_EXPERTS), 0.01)
    exp_w1 = nrm(ks[27], (DEPTH, N_EXPERTS, D_MODEL, D_EXPERT), D_MODEL ** -0.5)
    exp_w3 = nrm(ks[28], (DEPTH, N_EXPERTS, D_MODEL, D_EXPERT), D_MODEL ** -0.5)
    exp_w2 = nrm(ks[29], (DEPTH, N_EXPERTS, D_EXPERT, D_MODEL), D_EXPERT ** -0.5 * beta)
    ln2_g = 1.0 + nrm(ks[30], (DEPTH, D_MODEL), 0.01)
    ln2_b = nrm(ks[31], (DEPTH, D_MODEL), 0.01)
    return {"x_prompt": x_prompt, "x_sample": x_sample, "w_in": w_in, "b_in": b_in,
            "conv_w": conv_w, "conv_b": conv_b, "lam_q1": lam_q1, "lam_k1": lam_k1,
            "lam_q2": lam_q2, "lam_k2": lam_k2, "subln_g": subln_g,
            "filt_w1": filt_w1, "filt_b1": filt_b1, "filt_freq1": filt_freq1,
            "filt_w2": filt_w2, "filt_b2": filt_b2, "filt_freq2": filt_freq2,
            "filt_w3": filt_w3, "hyena_d": hyena_d, "w_out": w_out, "b_out": b_out,
            "ln1_g": ln1_g, "ln1_b": ln1_b, "router_group_w": router_group_w,
            "router_group_b": router_group_b, "router_expert_w": router_expert_w,
            "router_expert_b": router_expert_b, "exp_w1": exp_w1, "exp_w3": exp_w3,
            "exp_w2": exp_w2, "ln2_g": ln2_g, "ln2_b": ln2_b}


def reference(x_prompt, x_sample, w_in, b_in, conv_w, conv_b, lam_q1, lam_k1, lam_q2, lam_k2, subln_g,
              filt_w1, filt_b1, filt_freq1, filt_w2, filt_b2, filt_freq2, filt_w3, hyena_d,
              w_out, b_out, ln1_g, ln1_b, router_group_w, router_group_b, router_expert_w,
              router_expert_b, exp_w1, exp_w3, exp_w2, ln2_g, ln2_b):
    params = (w_in, b_in, conv_w, conv_b, lam_q1, lam_k1, lam_q2, lam_k2, subln_g,
              filt_w1, filt_b1, filt_freq1, filt_w2, filt_b2, filt_freq2, filt_w3, hyena_d,
              w_out, b_out, ln1_g, ln1_b, router_group_w, router_group_b, router_expert_w,
              router_expert_b, exp_w1, exp_w3, exp_w2, ln2_g, ln2_b)
    y_prompt = trunk(x_prompt, params)
    y_sample = trunk(x_sample, params)
    return (y_prompt, y_sample)
```

```python
import functools
import math

import numpy as np
import jax
import jax.numpy as jnp
from jax import lax
from jax.experimental import pallas as pl
from jax.experimental.pallas import tpu as pltpu

F32 = jnp.float32
BF16 = jnp.bfloat16

ATT_HEADS = 4
ATT_HEAD_DIM = 64
ATT_V_DIM = 2 * ATT_HEAD_DIM
ATT_QK_WIDTH = ATT_HEADS * 2 * ATT_HEAD_DIM
ATT_WIDTH = ATT_HEADS * ATT_V_DIM
FILTER_EMB = 33
FILTER_BANDS = (FILTER_EMB - 1) // 2
DECAY_TARGET = 1e-2
FAST_DECAY_PCT = 0.3
SLOW_DECAY_PCT = 1.5
FILTER_SHIFT = 0.05
N_GROUPS = 4
EXPERTS_PER_GROUP = 4
N_EXPERTS = N_GROUPS * EXPERTS_PER_GROUP
LN_EPS = 1e-5
RMS_EPS = 1e-5

LANES = 128
BF16_SUBLANES = 16
VMEM_LIMIT_BYTES = 56 * 1024 * 1024

ATT_TILE = 256
VT_ROWS = ATT_V_DIM + BF16_SUBLANES
PROJ_TILE = 512
GATE_TILE = 512
FILTER_TILE = 512
DFT_N2 = 128
DFT_LANE_BLOCK = 2048
DFT_K1_BLOCK = 8
MOE_TILE = 512
ROUTER_LANES = LANES
NEG_BIG = -1e30


def _params(*sem):
    return pltpu.CompilerParams(dimension_semantics=sem, vmem_limit_bytes=VMEM_LIMIT_BYTES)


def _full(a):
    nd = a.ndim
    return pl.BlockSpec(a.shape, lambda *_: (0,) * nd)


def _dot(a, b):
    return jnp.dot(a, b, preferred_element_type=F32)


def _dot_nt(a, b):
    return lax.dot_general(a, b, (((1,), (1,)), ((), ())), preferred_element_type=F32)


def _split_bf16(x):
    hi = x.astype(BF16)
    lo = (x - hi.astype(F32)).astype(BF16)
    return hi, lo


def _dot_3pass(a, b):
    ah, al = _split_bf16(a)
    bh, bl = _split_bf16(b)
    return _dot(ah, bh) + (_dot(ah, bl) + _dot(al, bh))


def _layer_norm(r, g, b):
    mu = jnp.mean(r, axis=-1, keepdims=True)
    c = r - mu
    var = jnp.mean(c * c, axis=-1, keepdims=True)
    return c * lax.rsqrt(var + LN_EPS) * g + b


def _inproj_kernel(x_ref, wq_ref, bq_ref, wk_ref, bk_ref, wvt_ref, bvt_ref, why_ref, bhy_ref,
                   q_ref, k_ref, vt_ref, phy_ref, *, n_sub, tk):
    xb = x_ref[...].astype(BF16)
    q_ref[...] = (_dot(xb, wq_ref[...]) + bq_ref[...]).astype(BF16)
    k_ref[...] = (_dot(xb, wk_ref[...]) + bk_ref[...]).astype(BF16)
    vt = (_dot_nt(wvt_ref[...], xb) + bvt_ref[...]).astype(BF16)
    row = lax.broadcasted_iota(jnp.int32, (BF16_SUBLANES, tk), 0)
    ones_rows = jnp.where(row == 0, 1.0, 0.0).astype(BF16)
    for s in range(n_sub):
        for h in range(ATT_HEADS):
            vt_ref[s, h, :ATT_V_DIM, :] = vt[h * ATT_V_DIM:(h + 1) * ATT_V_DIM, s * tk:(s + 1) * tk]
            vt_ref[s, h, ATT_V_DIM:, :] = ones_rows
    phy_ref[...] = _dot(xb, why_ref[...]) + bhy_ref[...]


def _inproj(x, wq, bq, wk, bk, wvt, bvt, why, bhy):
    B, L, D = x.shape
    tm, tk = PROJ_TILE, ATT_TILE
    n_sub = tm // tk
    hyw = why.shape[1]
    tok = lambda w: pl.BlockSpec((None, tm, w), lambda b, i: (b, i, 0))
    return pl.pallas_call(
        functools.partial(_inproj_kernel, n_sub=n_sub, tk=tk),
        grid=(B, L // tm),
        in_specs=[tok(D)] + [_full(a) for a in (wq, bq, wk, bk, wvt, bvt, why, bhy)],
        out_specs=(tok(ATT_QK_WIDTH), tok(ATT_QK_WIDTH),
                   pl.BlockSpec((None, n_sub, ATT_HEADS, VT_ROWS, tk), lambda b, i: (b, i, 0, 0, 0)),
                   tok(hyw)),
        out_shape=(jax.ShapeDtypeStruct((B, L, ATT_QK_WIDTH), BF16),
                   jax.ShapeDtypeStruct((B, L, ATT_QK_WIDTH), BF16),
                   jax.ShapeDtypeStruct((B, L // tk, ATT_HEADS, VT_ROWS, tk), BF16),
                   jax.ShapeDtypeStruct((B, L, hyw), F32)),
        compiler_params=_params("parallel", "parallel"),
        name="inproj",
    )(x, wq, bq, wk, bk, wvt, bvt, why, bhy)


def _attn_kernel(slopes_ref, q_ref, k_ref, vt_ref, dpos_ref, dneg_ref, ddiag_ref, lam_ref, g_ref,
                 o_ref, acc_ref, *, t, nk, lam_init):
    h = pl.program_id(1)
    qi = pl.program_id(2)
    slope = slopes_ref[h]
    q = q_ref[...]
    lane = lax.broadcasted_iota(jnp.int32, q.shape, 1)
    zero = jnp.zeros_like(q)
    qmaps = (jnp.where(lane < ATT_HEAD_DIM, q, zero), jnp.where(lane >= ATT_HEAD_DIM, q, zero))
    acc_ref[...] = jnp.zeros_like(acc_ref)

    def tile(kj, dist_ref, ms):
        kt = k_ref[pl.ds(pl.multiple_of(kj * t, t), t), :]
        vt = vt_ref[kj]
        c = -slope * (jnp.abs(kj - qi) * t).astype(F32)
        dist = dist_ref[...]
        out = []
        for mi in range(2):
            s = _dot_nt(kt, qmaps[mi]) + dist
            m_new = jnp.maximum(ms[mi], jnp.max(s, axis=0, keepdims=True) + c)
            e = jnp.exp(s - (m_new - c)).astype(BF16)
            acc_ref[mi] = acc_ref[mi] * jnp.exp(ms[mi] - m_new) + _dot(vt, e)
            out.append(m_new)
        return tuple(out)

    m0 = jnp.full((1, t), NEG_BIG, F32)
    ms = lax.fori_loop(0, qi, lambda kj, ms: tile(kj, dpos_ref, ms), (m0, m0))
    ms = tile(qi, ddiag_ref, ms)
    lax.fori_loop(qi + 1, nk, lambda kj, ms: tile(kj, dneg_ref, ms), ms)

    a0 = acc_ref[0]
    a1 = acc_ref[1]
    o0 = a0[:ATT_V_DIM] / a0[ATT_V_DIM:ATT_V_DIM + 1]
    o1 = a1[:ATT_V_DIM] / a1[ATT_V_DIM:ATT_V_DIM + 1]
    lv = lam_ref[...]
    lam = (jnp.exp(jnp.sum(lv[0:1] * lv[1:2], axis=-1, keepdims=True))
           - jnp.exp(jnp.sum(lv[2:3] * lv[3:4], axis=-1, keepdims=True)) + lam_init)
    d = o0 - lam * o1
    d = d * lax.rsqrt(jnp.mean(d * d, axis=0, keepdims=True) + RMS_EPS)
    o_ref[...] = (d.T * (g_ref[...] * (1.0 - lam_init))).astype(o_ref.dtype)


def _attention(q, k, vt, slopes, dpos, dneg, ddiag, lamv, g, *, lam_init):
    B, L, _ = q.shape
    t = ATT_TILE
    nk = L // t
    head_blk = lambda rows: pl.BlockSpec((None, rows, LANES), lambda b, h, i: (b, i if rows == t else 0, h))
    dist_blk = pl.BlockSpec((None, t, t), lambda b, h, i: (h, 0, 0))
    return pl.pallas_call(
        functools.partial(_attn_kernel, t=t, nk=nk, lam_init=lam_init),
        grid=(B, ATT_HEADS, nk),
        in_specs=[pl.BlockSpec(memory_space=pltpu.SMEM),
                  head_blk(t), head_blk(L),
                  pl.BlockSpec((None, nk, None, VT_ROWS, t), lambda b, h, i: (b, 0, h, 0, 0)),
                  dist_blk, dist_blk, dist_blk, _full(lamv), _full(g)],
        out_specs=head_blk(t),
        out_shape=jax.ShapeDtypeStruct((B, L, ATT_WIDTH), BF16),
        scratch_shapes=[pltpu.VMEM((2, VT_ROWS, t), F32)],
        compiler_params=_params("parallel", "parallel", "arbitrary"),
        name="diff_attention",
    )(slopes, q, k, vt, dpos, dneg, ddiag, lamv, g)


def _hygate_kernel(p_ref, prev_ref, next_ref, w_ref, b_ref, u_ref, x0_ref, *, tl, c):
    i = pl.program_id(1)
    last = pl.num_programs(1) - 1
    x = p_ref[...]
    sub = prev_ref.shape[0]
    prev_row = jnp.where(i == 0, 0.0, prev_ref[sub - 1:sub, :])
    next_row = jnp.where(i == last, 0.0, next_ref[0:1, :])
    row = lax.broadcasted_iota(jnp.int32, x.shape, 0)
    xm = jnp.where(row == 0, prev_row, pltpu.roll(x, 1, 0))
    xp = jnp.where(row == tl - 1, next_row, pltpu.roll(x, tl - 1, 0))
    w = w_ref[...]
    hy = xm * w[0:1] + x * w[1:2] + xp * w[2:3] + b_ref[...]
    x0_ref[...] = hy[:, :c]
    u_ref[...] = hy[:, 2 * c:] * hy[:, c:2 * c]


def _hygate(phy, conv_w, conv_b):
    B, L, W = phy.shape
    c = W // 3
    tl = GATE_TILE
    sub = 8
    nb = tl // sub
    last_blk = L // sub - 1
    return pl.pallas_call(
        functools.partial(_hygate_kernel, tl=tl, c=c),
        grid=(B, L // tl),
        in_specs=[pl.BlockSpec((None, tl, W), lambda b, i: (b, i, 0)),
                  pl.BlockSpec((None, sub, W), lambda b, i: (b, jnp.maximum(i * nb - 1, 0), 0)),
                  pl.BlockSpec((None, sub, W), lambda b, i: (b, jnp.minimum((i + 1) * nb, last_blk), 0)),
                  _full(conv_w), _full(conv_b)],
        out_specs=(pl.BlockSpec((None, tl, c), lambda b, i: (b, i, 0)),
                   pl.BlockSpec((None, tl, c), lambda b, i: (b, i, 0))),
        out_shape=(jax.ShapeDtypeStruct((B, L, c), F32), jax.ShapeDtypeStruct((B, L, c), F32)),
        compiler_params=_params("parallel", "parallel"),
        name="hyena_gate",
    )(phy, phy, phy, conv_w, conv_b)


def _filter_kernel(z_ref, w1_ref, b1_ref, f1_ref, w2_ref, b2_ref, f2_ref, w3_ref, deltas_ref,
                   hf_ref, hb_ref, *, tl, seq, c):
    i = pl.program_id(0)
    h = jnp.sin(f1_ref[...] * (_dot_3pass(z_ref[...], w1_ref[...]) + b1_ref[...]))
    h = jnp.sin(f2_ref[...] * (_dot_3pass(h, w2_ref[...]) + b2_ref[...]))
    h = _dot_3pass(h, w3_ref[...])
    pos = (lax.broadcasted_iota(jnp.int32, (tl, c), 0) + i * tl).astype(F32)
    decay = jnp.exp(-(pos * (1.0 / (seq - 1))) * deltas_ref[...]) + FILTER_SHIFT
    hf_ref[...] = h[:, :c] * decay
    hb_ref[...] = h[:, c:] * decay


def _filters(z, w1, b1, f1, w2, b2, f2, w3, deltas):
    L = z.shape[0]
    c = w3.shape[1] // 2
    tl = FILTER_TILE
    return pl.pallas_call(
        functools.partial(_filter_kernel, tl=tl, seq=L, c=c),
        grid=(L // tl,),
        in_specs=[pl.BlockSpec((tl, z.shape[1]), lambda i: (i, 0))]
        + [_full(a) for a in (w1, b1, f1, w2, b2, f2, w3, deltas)],
        out_specs=(pl.BlockSpec((tl, c), lambda i: (i, 0)), pl.BlockSpec((tl, c), lambda i: (i, 0))),
        out_shape=(jax.ShapeDtypeStruct((L, c), F32), jax.ShapeDtypeStruct((L, c), F32)),
        compiler_params=_params("parallel"),
        name="hyena_filter",
    )(z, w1, b1, f1, w2, b2, f2, w3, deltas)


def _dft_outer_kernel(u_ref, m_ref, ar_ref, ai_ref, *, n1):
    a = _dot(m_ref[...], u_ref[...].astype(BF16))
    ar_ref[...] = a[:n1]
    ai_ref[...] = a[n1:]


def _dft_outer(u2, m_outer):
    B, n1_in, W = u2.shape
    n1 = m_outer.shape[0] // 2
    lb = min(DFT_LANE_BLOCK, W)
    blk = lambda rows: pl.BlockSpec((None, rows, lb), lambda b, j: (b, 0, j))
    return pl.pallas_call(
        functools.partial(_dft_outer_kernel, n1=n1),
        grid=(B, W // lb),
        in_specs=[blk(n1_in), _full(m_outer)],
        out_specs=(blk(n1), blk(n1)),
        out_shape=(jax.ShapeDtypeStruct((B, n1, W), F32), jax.ShapeDtypeStruct((B, n1, W), F32)),
        compiler_params=_params("parallel", "parallel"),
        name="dft_outer",
    )(u2, m_outer)


def _twiddle(ar, ai, cph, sph, conj):
    if conj:
        return ar * cph - ai * sph, ai * cph + ar * sph
    return ar * cph + ai * sph, ai * cph - ar * sph


def _dft_spectrum_kernel(ar_ref, ai_ref, twc_ref, tws_ref, mf_ref, kf_ref, *, kb, n2, scale):
    for r in range(kb):
        pr, pi = _twiddle(ar_ref[r], ai_ref[r], twc_ref[r], tws_ref[r], conj=False)
        a = jnp.concatenate([pr.astype(BF16), pi.astype(BF16)], axis=0)
        kf_ref[r] = _dot(mf_ref[...], a) * scale


def _dft_spectrum(ar, ai, twc, tws, m_fwd, *, scale):
    n1, n2, c = ar.shape
    kb = DFT_K1_BLOCK
    blk = lambda rows, w: pl.BlockSpec((kb, rows, w), lambda i: (i, 0, 0))
    return pl.pallas_call(
        functools.partial(_dft_spectrum_kernel, kb=kb, n2=n2, scale=scale),
        grid=(n1 // kb,),
        in_specs=[blk(n2, c), blk(n2, c), blk(n2, 1), blk(n2, 1), _full(m_fwd)],
        out_specs=blk(2 * n2, c),
        out_shape=jax.ShapeDtypeStruct((n1, 2 * n2, c), F32),
        compiler_params=_params("parallel"),
        name="dft_filter_spectrum",
    )(ar, ai, twc, tws, m_fwd)


def _dft_inner_kernel(ar_ref, ai_ref, twc_ref, tws_ref, kf_ref, mf_ref, mi_ref, br_ref, bi_ref, *, kb, n2):
    for r in range(kb):
        cph = twc_ref[r]
        sph = tws_ref[r]
        pr, pi = _twiddle(ar_ref[r], ai_ref[r], cph, sph, conj=False)
        a = jnp.concatenate([pr.astype(BF16), pi.astype(BF16)], axis=0)
        x = _dot(mf_ref[...], a)
        xr, xi = x[:n2], x[n2:]
        kr, ki = kf_ref[r, :n2], kf_ref[r, n2:]
        y = jnp.concatenate([(xr * kr - xi * ki).astype(BF16), (xr * ki + xi * kr).astype(BF16)], axis=0)
        bb = _dot(mi_ref[...], y)
        qr, qi = _twiddle(bb[:n2], bb[n2:], cph, sph, conj=True)
        br_ref[r] = qr
        bi_ref[r] = qi


def _dft_inner(ar, ai, twc, tws, kf, m_fwd, m_inv):
    B, n1, n2, c = ar.shape
    kb = DFT_K1_BLOCK
    sig = pl.BlockSpec((None, kb, n2, c), lambda i, b: (b, i, 0, 0))
    tab = lambda rows, w: pl.BlockSpec((kb, rows, w), lambda i, b: (i, 0, 0))
    return pl.pallas_call(
        functools.partial(_dft_inner_kernel, kb=kb, n2=n2),
        grid=(n1 // kb, B),
        in_specs=[sig, sig, tab(n2, 1), tab(n2, 1), tab(2 * n2, c), _full(m_fwd), _full(m_inv)],
        out_specs=(sig, sig),
        out_shape=(jax.ShapeDtypeStruct(ar.shape, F32), jax.ShapeDtypeStruct(ar.shape, F32)),
        compiler_params=_params("parallel", "parallel"),
        name="dft_inner",
    )(ar, ai, twc, tws, kf, m_fwd, m_inv)


def _dft_outer_inv_kernel(br_ref, bi_ref, m_ref, u_ref, x0_ref, d_ref, o_ref):
    b = jnp.concatenate([br_ref[...].astype(BF16), bi_ref[...].astype(BF16)], axis=0)
    y = _dot(m_ref[...], b)
    u = u_ref[...]
    o_ref[...] = (x0_ref[...] * (y + u * d_ref[...])).astype(o_ref.dtype)


def _dft_outer_inv(br, bi, m_inv_outer, u2, x02, d_lanes):
    B, n1, W = br.shape
    lb = d_lanes.shape[1]
    blk = lambda rows: pl.BlockSpec((None, rows, lb), lambda b, j: (b, 0, j))
    return pl.pallas_call(
        _dft_outer_inv_kernel,
        grid=(B, W // lb),
        in_specs=[blk(n1), blk(n1), _full(m_inv_outer), blk(n1 // 2), blk(n1 // 2), _full(d_lanes)],
        out_specs=blk(n1 // 2),
        out_shape=jax.ShapeDtypeStruct((B, n1 // 2, W), BF16),
        compiler_params=_params("parallel", "parallel"),
        name="dft_outer_inv_gate",
    )(br, bi, m_inv_outer, u2, x02, d_lanes)


@functools.lru_cache(maxsize=None)
def _dft_tables(seq):
    n = 2 * seq
    n2 = DFT_N2
    n1 = n // n2
    k1 = np.arange(n1)[:, None]
    th = 2.0 * np.pi * ((k1 * np.arange(n1)[None, :]) % n1) / n1
    c1, s1 = np.cos(th), np.sin(th)
    m_outer_full = np.concatenate([c1, -s1], axis=0)
    m_outer_half = m_outer_full[:, :n1 // 2]
    m_outer_inv = np.concatenate([c1[:n1 // 2], -s1[:n1 // 2]], axis=1)
    ps = 2.0 * np.pi * ((np.arange(n2)[:, None] * np.arange(n2)[None, :]) % n2) / n2
    c2, s2 = np.cos(ps), np.sin(ps)
    m_fwd = np.block([[c2, s2], [-s2, c2]])
    m_inv = np.block([[c2, -s2], [s2, c2]])
    ph = 2.0 * np.pi * (k1 * np.arange(n2)[None, :]) / n
    f = lambda a: np.asarray(a, np.float32)
    return dict(n1=n1, n2=n2, m_outer_full=f(m_outer_full), m_outer_half=f(m_outer_half),
                m_outer_inv=f(m_outer_inv), m_fwd=f(m_fwd), m_inv=f(m_inv),
                twc=f(np.cos(ph))[:, :, None], tws=f(np.sin(ph))[:, :, None])


def _filter_features(seq):
    t = jnp.linspace(0.0, 1.0, seq, dtype=F32)[:, None]
    w = 2.0 * math.pi * jnp.arange(seq, dtype=F32) / seq
    f = jnp.linspace(1e-4, FILTER_BANDS - 1, FILTER_BANDS, dtype=F32)
    ang = w[:, None] * f[None, :]
    z = jnp.concatenate([t, jnp.cos(ang), -jnp.sin(ang)], axis=-1)
    return jnp.pad(z, ((0, 0), (0, LANES - FILTER_EMB)))


def _hyena(phy, lp):
    B, L, W = phy.shape
    c = W // 3
    tb = _dft_tables(L)
    n1, n2 = tb["n1"], tb["n2"]
    bf = lambda name: jnp.asarray(tb[name]).astype(BF16)
    twc, tws = jnp.asarray(tb["twc"]), jnp.asarray(tb["tws"])

    u, x0 = _hygate(phy, lp["conv_w"], lp["conv_b"])

    hf, hb = _filters(_filter_features(L), lp["filt_w1"], lp["filt_b1"], lp["filt_f1"], lp["filt_w2"],
                      lp["filt_b2"], lp["filt_f2"], lp["filt_w3"], lp["deltas"])
    kern = jnp.concatenate([hf, jnp.zeros((1, c), F32), hb[:0:-1]], axis=0)
    far, fai = _dft_outer(kern.reshape(1, n1, n2 * c), bf("m_outer_full"))
    kf = _dft_spectrum(far.reshape(n1, n2, c), fai.reshape(n1, n2, c), twc, tws, bf("m_fwd"),
                       scale=1.0 / (2 * L))

    u2 = u.reshape(B, n1 // 2, n2 * c)
    ar, ai = _dft_outer(u2, bf("m_outer_half"))
    br, bi = _dft_inner(ar.reshape(B, n1, n2, c), ai.reshape(B, n1, n2, c), twc, tws, kf,
                        bf("m_fwd"), bf("m_inv"))
    lb = min(DFT_LANE_BLOCK, n2 * c)
    d_lanes = jnp.tile(lp["hyena_d"], (1, lb // c))
    hyo = _dft_outer_inv(br.reshape(B, n1, n2 * c), bi.reshape(B, n1, n2 * c), bf("m_outer_inv"),
                         u2, x0.reshape(B, n1 // 2, n2 * c), d_lanes)
    return hyo.reshape(B, L, c)


def _outproj_kernel(att_ref, hyo_ref, x_ref, wa_ref, wh_ref, bo_ref, g_ref, b_ref, wr_hi_ref, wr_lo_ref,
                    br_ref, x1_ref, comb_ref, *, alpha):
    m = _dot(att_ref[...], wa_ref[...]) + _dot(hyo_ref[...], wh_ref[...]) + bo_ref[...]
    x1 = _layer_norm(alpha * x_ref[...] + m, g_ref[...], b_ref[...])
    x1_ref[...] = x1

    xh, xl = _split_bf16(x1)
    lg = _dot(xh, wr_hi_ref[...]) + (_dot(xh, wr_lo_ref[...]) + _dot(xl, wr_hi_ref[...])) + br_ref[...]
    lane = lax.broadcasted_iota(jnp.int32, lg.shape, 1)
    big = jnp.int32(ROUTER_LANES)
    ninf = jnp.float32(-jnp.inf)
    first = lambda mask: jnp.min(jnp.where(mask, lane, big), axis=-1, keepdims=True)

    gmask = lane < N_GROUPS
    gmax = jnp.max(jnp.where(gmask, lg, ninf), axis=-1, keepdims=True)
    gsum = jnp.sum(jnp.where(gmask, jnp.exp(lg - gmax), 0.0), axis=-1, keepdims=True)
    g_w = 1.0 / gsum
    g_idx = first(gmask & (lg == gmax))
    lo = N_GROUPS + EXPERTS_PER_GROUP * g_idx
    emask = (lane >= lo) & (lane < lo + EXPERTS_PER_GROUP)
    e1 = jnp.max(jnp.where(emask, lg, ninf), axis=-1, keepdims=True)
    i1 = first(emask & (lg == e1))
    rest = emask & (lane != i1)
    e2 = jnp.max(jnp.where(rest, lg, ninf), axis=-1, keepdims=True)
    i2 = first(rest & (lg == e2))
    tt = jnp.exp(e2 - e1)
    w1 = g_w / (1.0 + tt)
    comb_ref[...] = jnp.where(lane == i1, w1, 0.0) + jnp.where(lane == i2, w1 * tt, 0.0)


def _outproj_ln_router(att, hyo, x, wa, wh, bo, g, b, wr_hi, wr_lo, br, *, alpha):
    B, L, D = x.shape
    tm = PROJ_TILE
    tok = lambda w: pl.BlockSpec((None, tm, w), lambda bb, i: (bb, i, 0))
    return pl.pallas_call(
        functools.partial(_outproj_kernel, alpha=alpha),
        grid=(B, L // tm),
        in_specs=[tok(att.shape[2]), tok(hyo.shape[2]), tok(D)]
        + [_full(a) for a in (wa, wh, bo, g, b, wr_hi, wr_lo, br)],
        out_specs=(tok(D), tok(ROUTER_LANES)),
        out_shape=(jax.ShapeDtypeStruct((B, L, D), F32), jax.ShapeDtypeStruct((B, L, ROUTER_LANES), F32)),
        compiler_params=_params("parallel", "parallel"),
        name="outproj_ln_router",
    )(att, hyo, x, wa, wh, bo, g, b, wr_hi, wr_lo, br)


def _moe_kernel(x_ref, comb_ref, w1_ref, w3_ref, w2_ref, g_ref, b_ref, o_ref, acc_ref, xb_ref, *, alpha):
    e = pl.program_id(2)

    @pl.when(e == 0)
    def _():
        acc_ref[...] = jnp.zeros_like(acc_ref)
        xb_ref[...] = x_ref[...].astype(BF16)

    xb = xb_ref[...]
    h = jax.nn.silu(_dot(xb, w1_ref[...])) * _dot(xb, w3_ref[...])
    y = _dot(h.astype(BF16), w2_ref[...])
    comb = comb_ref[...]
    lane = lax.broadcasted_iota(jnp.int32, comb.shape, 1)
    ce = jnp.sum(jnp.where(lane == N_GROUPS + e, comb, 0.0), axis=-1, keepdims=True)
    acc_ref[...] += ce * y

    @pl.when(e == pl.num_programs(2) - 1)
    def _():
        o_ref[...] = _layer_norm(alpha * x_ref[...] + acc_ref[...], g_ref[...], b_ref[...])


def _moe_ln(x1, comb, w1, w3, w2, g, b, *, alpha):
    B, L, D = x1.shape
    E, _, de = w1.shape
    tm = MOE_TILE
    tok = lambda w: pl.BlockSpec((None, tm, w), lambda bb, i, e: (bb, i, 0))
    return pl.pallas_call(
        functools.partial(_moe_kernel, alpha=alpha),
        grid=(B, L // tm, E),
        in_specs=[tok(D), tok(ROUTER_LANES),
                  pl.BlockSpec((None, D, de), lambda bb, i, e: (e, 0, 0)),
                  pl.BlockSpec((None, D, de), lambda bb, i, e: (e, 0, 0)),
                  pl.BlockSpec((None, de, D), lambda bb, i, e: (e, 0, 0)),
                  _full(g), _full(b)],
        out_specs=tok(D),
        out_shape=jax.ShapeDtypeStruct((B, L, D), F32),
        scratch_shapes=[pltpu.VMEM((tm, D), F32), pltpu.VMEM((tm, D), BF16)],
        compiler_params=_params("parallel", "parallel", "arbitrary"),
        name="moe_ln",
    )(x1, comb, w1, w3, w2, g, b)


def _alibi_tables():
    t = ATT_TILE
    slopes = jnp.asarray(2.0 ** (-8.0 * np.arange(1, ATT_HEADS + 1) / ATT_HEADS), dtype=F32)
    off = (jnp.arange(t)[:, None] - jnp.arange(t)[None, :]).astype(F32)
    dpos = slopes[:, None, None] * off[None]
    return slopes, dpos, -dpos, -jnp.abs(dpos)


def _prep_layer(l, depth, w_in, b_in, conv_w, conv_b, lam_q1, lam_k1, lam_q2, lam_k2, subln_g,
                filt_w1, filt_b1, filt_freq1, filt_w2, filt_b2, filt_freq2, filt_w3, hyena_d,
                w_out, b_out, ln1_g, ln1_b, router_group_w, router_group_b, router_expert_w,
                router_expert_b, exp_w1, exp_w3, exp_w2, ln2_g, ln2_b):
    a0, a1, a2 = ATT_QK_WIDTH, 2 * ATT_QK_WIDTH, 2 * ATT_QK_WIDTH + ATT_WIDTH
    scale = ATT_HEAD_DIM ** -0.5
    row = lambda v: v.reshape(1, -1)
    w, b = w_in[l], b_in[l]
    c = hyena_d.shape[1]
    wr = jnp.concatenate([router_group_w[l], router_expert_w[l]], axis=1)
    wr = jnp.pad(wr, ((0, 0), (0, ROUTER_LANES - wr.shape[1])))
    wr_hi = wr.astype(BF16)
    br = jnp.concatenate([router_group_b[l], router_expert_b[l]])
    max_decay = abs(math.log(DECAY_TARGET) / FAST_DECAY_PCT)
    min_decay = abs(math.log(DECAY_TARGET) / SLOW_DECAY_PCT)
    return dict(
        lam_init=0.8 - 0.6 * math.exp(-0.3 * l),
        alpha=(2.0 * depth) ** 0.25,
        wq=(w[:, :a0] * scale).astype(BF16), bq=row(b[:a0] * scale),
        wk=w[:, a0:a1].astype(BF16), bk=row(b[a0:a1]),
        wvt=w[:, a1:a2].T.astype(BF16), bvt=b[a1:a2].reshape(-1, 1),
        why=w[:, a2:].astype(BF16), bhy=row(b[a2:]),
        conv_w=conv_w[l], conv_b=row(conv_b[l]),
        lamv=jnp.stack([lam_q1[l], lam_k1[l], lam_q2[l], lam_k2[l]]).astype(F32),
        subln_g=row(subln_g[l]),
        filt_w1=jnp.pad(filt_w1[l], ((0, LANES - FILTER_EMB), (0, 0))), filt_b1=row(filt_b1[l]),
        filt_f1=row(filt_freq1[l]), filt_w2=filt_w2[l], filt_b2=row(filt_b2[l]), filt_f2=row(filt_freq2[l]),
        filt_w3=filt_w3[l],
        deltas=row(jnp.linspace(min_decay, max_decay, c, dtype=F32)),
        hyena_d=row(hyena_d[l]),
        wa=w_out[l][:ATT_WIDTH].astype(BF16), wh=w_out[l][ATT_WIDTH:].astype(BF16), bo=row(b_out[l]),
        ln1_g=row(ln1_g[l]), ln1_b=row(ln1_b[l]),
        wr_hi=wr_hi, wr_lo=(wr - wr_hi.astype(F32)).astype(BF16),
        br=row(jnp.pad(br, (0, ROUTER_LANES - br.shape[0]))),
        w1=exp_w1[l].astype(BF16), w3=exp_w3[l].astype(BF16), w2=exp_w2[l].astype(BF16),
        ln2_g=row(ln2_g[l]), ln2_b=row(ln2_b[l]),
    )


def _layer(x, lp, alibi):
    slopes, dpos, dneg, ddiag = alibi
    q, k, vt, phy = _inproj(x, lp["wq"], lp["bq"], lp["wk"], lp["bk"], lp["wvt"], lp["bvt"],
                            lp["why"], lp["bhy"])
    att = _attention(q, k, vt, slopes, dpos, dneg, ddiag, lp["lamv"], lp["subln_g"],
                     lam_init=lp["lam_init"])
    hyo = _hyena(phy, lp)
    x1, comb = _outproj_ln_router(att, hyo, x, lp["wa"], lp["wh"], lp["bo"], lp["ln1_g"], lp["ln1_b"],
                                  lp["wr_hi"], lp["wr_lo"], lp["br"], alpha=lp["alpha"])
    return _moe_ln(x1, comb, lp["w1"], lp["w3"], lp["w2"], lp["ln2_g"], lp["ln2_b"], alpha=lp["alpha"])


def kernel(x_prompt, x_sample, w_in, b_in, conv_w, conv_b, lam_q1, lam_k1, lam_q2, lam_k2, subln_g, filt_w1, filt_b1, filt_freq1, filt_w2, filt_b2, filt_freq2, filt_w3, hyena_d, w_out, b_out, ln1_g, ln1_b, router_group_w, router_group_b, router_expert_w, router_expert_b, exp_w1, exp_w3, exp_w2, ln2_g, ln2_b):
    params = (w_in, b_in, conv_w, conv_b, lam_q1, lam_k1, lam_q2, lam_k2, subln_g, filt_w1, filt_b1,
              filt_freq1, filt_w2, filt_b2, filt_freq2, filt_w3, hyena_d, w_out, b_out, ln1_g, ln1_b,
              router_group_w, router_group_b, router_expert_w, router_expert_b, exp_w1, exp_w3, exp_w2,
              ln2_g, ln2_b)
    depth = w_in.shape[0]
    layers = [_prep_layer(l, depth, *params) for l in range(depth)]
    alibi = _alibi_tables()

    def trunk(x):
        for lp in layers:
            x = _layer(x, lp, alibi)
        return x

    return (trunk(x_prompt), trunk(x_sample))
```

```python
import functools
import math

import numpy as np
import jax
import jax.numpy as jnp
from jax import lax
from jax.experimental import pallas as pl
from jax.experimental.pallas import tpu as pltpu

F32 = jnp.float32
BF16 = jnp.bfloat16

ATT_HEADS = 4
ATT_HEAD_DIM = 64
ATT_V_DIM = 2 * ATT_HEAD_DIM
ATT_QK_WIDTH = ATT_HEADS * 2 * ATT_HEAD_DIM
ATT_WIDTH = ATT_HEADS * ATT_V_DIM
FILTER_EMB = 33
FILTER_BANDS = (FILTER_EMB - 1) // 2
DECAY_TARGET = 1e-2
FAST_DECAY_PCT = 0.3
SLOW_DECAY_PCT = 1.5
FILTER_SHIFT = 0.05
N_GROUPS = 4
EXPERTS_PER_GROUP = 4
N_EXPERTS = N_GROUPS * EXPERTS_PER_GROUP
LN_EPS = 1e-5
RMS_EPS = 1e-5

LANES = 128
BF16_SUBLANES = 16
VMEM_LIMIT_BYTES = 56 * 1024 * 1024

ATT_TILE = 256
ATT_KEY_BLOCK = 1024
VT_ROWS = ATT_V_DIM + BF16_SUBLANES
PROJ_TILE = 512
GATE_TILE = 512
FILTER_TILE = 512
DFT_N2 = 128
DFT_LANE_BLOCK = 2048
DFT_K1_BLOCK = 8
MOE_BLOCK = 1024
MOE_ROW_TILE = 128
MOE_GROUP_ALIGN = BF16_SUBLANES
ROUTER_LANES = LANES
NEG_BIG = -1e30


def _params(*sem):
    return pltpu.CompilerParams(dimension_semantics=sem, vmem_limit_bytes=VMEM_LIMIT_BYTES)


def _full(a):
    nd = a.ndim
    return pl.BlockSpec(a.shape, lambda *_: (0,) * nd)


def _dot(a, b):
    return jnp.dot(a, b, preferred_element_type=F32)


def _dot_nt(a, b):
    return lax.dot_general(a, b, (((1,), (1,)), ((), ())), preferred_element_type=F32)


def _split_bf16(x):
    hi = x.astype(BF16)
    lo = (x - hi.astype(F32)).astype(BF16)
    return hi, lo


def _dot_3pass(a, b):
    ah, al = _split_bf16(a)
    bh, bl = _split_bf16(b)
    return _dot(ah, bh) + (_dot(ah, bl) + _dot(al, bh))


def _layer_norm(r, g, b):
    mu = jnp.mean(r, axis=-1, keepdims=True)
    c = r - mu
    var = jnp.mean(c * c, axis=-1, keepdims=True)
    return c * lax.rsqrt(var + LN_EPS) * g + b


def _inproj_kernel(x_ref, wq_ref, bq_ref, wk_ref, bk_ref, wvt_ref, bvt_ref, why_ref, bhy_ref,
                   q_ref, k_ref, vt_ref, phy_ref):
    xb = x_ref[...].astype(BF16)
    tm = xb.shape[0]
    q_ref[...] = (_dot(xb, wq_ref[...]) + bq_ref[...]).astype(BF16)
    k_ref[...] = (_dot(xb, wk_ref[...]) + bk_ref[...]).astype(BF16)
    vt = (_dot_nt(wvt_ref[...], xb) + bvt_ref[...]).astype(BF16)
    row = lax.broadcasted_iota(jnp.int32, (BF16_SUBLANES, tm), 0)
    ones_rows = jnp.where(row == 0, 1.0, 0.0).astype(BF16)
    for h in range(ATT_HEADS):
        vt_ref[h, :ATT_V_DIM, :] = vt[h * ATT_V_DIM:(h + 1) * ATT_V_DIM]
        vt_ref[h, ATT_V_DIM:, :] = ones_rows
    phy_ref[...] = _dot(xb, why_ref[...]) + bhy_ref[...]


def _inproj(x, wq, bq, wk, bk, wvt, bvt, why, bhy):
    B, L, D = x.shape
    tm, kb = PROJ_TILE, min(ATT_KEY_BLOCK, L)
    per = kb // tm
    hyw = why.shape[1]
    tok = lambda w: pl.BlockSpec((None, tm, w), lambda b, i: (b, i, 0))
    return pl.pallas_call(
        _inproj_kernel,
        grid=(B, L // tm),
        in_specs=[tok(D)] + [_full(a) for a in (wq, bq, wk, bk, wvt, bvt, why, bhy)],
        out_specs=(tok(ATT_QK_WIDTH), tok(ATT_QK_WIDTH),
                   pl.BlockSpec((None, None, ATT_HEADS, VT_ROWS, tm),
                                lambda b, i: (b, i // per, 0, 0, i % per)),
                   tok(hyw)),
        out_shape=(jax.ShapeDtypeStruct((B, L, ATT_QK_WIDTH), BF16),
                   jax.ShapeDtypeStruct((B, L, ATT_QK_WIDTH), BF16),
                   jax.ShapeDtypeStruct((B, L // kb, ATT_HEADS, VT_ROWS, kb), BF16),
                   jax.ShapeDtypeStruct((B, L, hyw), F32)),
        compiler_params=_params("parallel", "parallel"),
        name="inproj",
    )(x, wq, bq, wk, bk, wvt, bvt, why, bhy)


def _attn_kernel(slopes_ref, q_ref, k_ref, vt_ref, dist_ref, lam_ref, g_ref,
                 o_ref, acc_ref, s_ref, *, t, kb, nkb, lam_init):
    h = pl.program_id(1)
    qi = pl.program_id(2)
    ns = kb // t
    slope = slopes_ref[h]
    q = q_ref[...]
    lane = lax.broadcasted_iota(jnp.int32, q.shape, 1)
    zero = jnp.zeros_like(q)
    qmaps = (jnp.where(lane < ATT_HEAD_DIM, q, zero), jnp.where(lane >= ATT_HEAD_DIM, q, zero))
    acc_ref[...] = jnp.zeros_like(acc_ref)

    def produce(j, slot):
        kblk = k_ref[pl.ds(pl.multiple_of(j * kb, kb), kb), :]
        for mi in range(2):
            s = _dot_nt(kblk, qmaps[mi])
            for st in range(ns):
                sel = jnp.clip(j * ns + st - qi, -1, 1) + 1
                s_ref[slot, mi, st * t:(st + 1) * t, :] = s[st * t:(st + 1) * t] + dist_ref[sel]

    def consume(j, slot, ms):
        vt = vt_ref[j]
        cs = [-slope * (jnp.abs(j * ns + st - qi) * t).astype(F32) for st in range(ns)]
        out = []
        for mi in range(2):
            sub = [s_ref[slot, mi, st * t:(st + 1) * t, :] for st in range(ns)]
            m_new = ms[mi]
            for st in range(ns):
                m_new = jnp.maximum(m_new, jnp.max(sub[st], axis=0, keepdims=True) + cs[st])
            e = jnp.concatenate([jnp.exp(sub[st] - (m_new - cs[st])).astype(BF16) for st in range(ns)],
                                axis=0)
            acc_ref[mi] = acc_ref[mi] * jnp.exp(ms[mi] - m_new) + _dot(vt, e)
            out.append(m_new)
        return tuple(out)

    m0 = jnp.full((1, t), NEG_BIG, F32)
    ms = (m0, m0)
    produce(0, 0)
    if nkb > 1:
        def pair(i, ms):
            ms = consume(2 * i, 0, ms)
            produce(2 * i + 1, 1)
            ms = consume(2 * i + 1, 1, ms)
            produce(2 * i + 2, 0)
            return ms

        ms = lax.fori_loop(0, nkb // 2 - 1, pair, ms)
        ms = consume(nkb - 2, 0, ms)
        produce(nkb - 1, 1)
        consume(nkb - 1, 1, ms)
    else:
        consume(0, 0, ms)

    a0 = acc_ref[0]
    a1 = acc_ref[1]
    o0 = a0[:ATT_V_DIM] / a0[ATT_V_DIM:ATT_V_DIM + 1]
    o1 = a1[:ATT_V_DIM] / a1[ATT_V_DIM:ATT_V_DIM + 1]
    lv = lam_ref[...]
    lam = (jnp.exp(jnp.sum(lv[0:1] * lv[1:2], axis=-1, keepdims=True))
           - jnp.exp(jnp.sum(lv[2:3] * lv[3:4], axis=-1, keepdims=True)) + lam_init)
    d = o0 - lam * o1
    d = d * lax.rsqrt(jnp.mean(d * d, axis=0, keepdims=True) + RMS_EPS)
    o_ref[...] = (d.T * (g_ref[...] * (1.0 - lam_init))).astype(o_ref.dtype)


def _attention(q, k, vt, slopes, dist, lamv, g, *, lam_init):
    B, L, _ = q.shape
    t = ATT_TILE
    nkb, kb = vt.shape[1], vt.shape[4]
    assert nkb == 1 or nkb % 2 == 0
    return pl.pallas_call(
        functools.partial(_attn_kernel, t=t, kb=kb, nkb=nkb, lam_init=lam_init),
        grid=(B, ATT_HEADS, L // t),
        in_specs=[pl.BlockSpec(memory_space=pltpu.SMEM),
                  pl.BlockSpec((None, t, LANES), lambda b, h, i: (b, i, h)),
                  pl.BlockSpec((None, L, LANES), lambda b, h, i: (b, 0, h)),
                  pl.BlockSpec((None, nkb, None, VT_ROWS, kb), lambda b, h, i: (b, 0, h, 0, 0)),
                  pl.BlockSpec((None, 3, t, t), lambda b, h, i: (h, 0, 0, 0)),
                  _full(lamv), _full(g)],
        out_specs=pl.BlockSpec((None, t, LANES), lambda b, h, i: (b, i, h)),
        out_shape=jax.ShapeDtypeStruct((B, L, ATT_WIDTH), BF16),
        scratch_shapes=[pltpu.VMEM((2, VT_ROWS, t), F32), pltpu.VMEM((2, 2, kb, t), F32)],
        compiler_params=_params("parallel", "parallel", "arbitrary"),
        name="diff_attention",
    )(slopes, q, k, vt, dist, lamv, g)


def _hygate_kernel(p_ref, prev_ref, next_ref, w_ref, b_ref, u_ref, x0_ref, *, tl, c):
    i = pl.program_id(1)
    last = pl.num_programs(1) - 1
    x = p_ref[...]
    sub = prev_ref.shape[0]
    prev_row = jnp.where(i == 0, 0.0, prev_ref[sub - 1:sub, :])
    next_row = jnp.where(i == last, 0.0, next_ref[0:1, :])
    row = lax.broadcasted_iota(jnp.int32, x.shape, 0)
    xm = jnp.where(row == 0, prev_row, pltpu.roll(x, 1, 0))
    xp = jnp.where(row == tl - 1, next_row, pltpu.roll(x, tl - 1, 0))
    w = w_ref[...]
    hy = xm * w[0:1] + x * w[1:2] + xp * w[2:3] + b_ref[...]
    x0_ref[...] = hy[:, :c]
    u_ref[...] = hy[:, 2 * c:] * hy[:, c:2 * c]


def _hygate(phy, conv_w, conv_b):
    B, L, W = phy.shape
    c = W // 3
    tl = GATE_TILE
    sub = 8
    nb = tl // sub
    last_blk = L // sub - 1
    return pl.pallas_call(
        functools.partial(_hygate_kernel, tl=tl, c=c),
        grid=(B, L // tl),
        in_specs=[pl.BlockSpec((None, tl, W), lambda b, i: (b, i, 0)),
                  pl.BlockSpec((None, sub, W), lambda b, i: (b, jnp.maximum(i * nb - 1, 0), 0)),
                  pl.BlockSpec((None, sub, W), lambda b, i: (b, jnp.minimum((i + 1) * nb, last_blk), 0)),
                  _full(conv_w), _full(conv_b)],
        out_specs=(pl.BlockSpec((None, tl, c), lambda b, i: (b, i, 0)),
                   pl.BlockSpec((None, tl, c), lambda b, i: (b, i, 0))),
        out_shape=(jax.ShapeDtypeStruct((B, L, c), F32), jax.ShapeDtypeStruct((B, L, c), F32)),
        compiler_params=_params("parallel", "parallel"),
        name="hyena_gate",
    )(phy, phy, phy, conv_w, conv_b)


def _filter_kernel(z_ref, w1_ref, b1_ref, f1_ref, w2_ref, b2_ref, f2_ref, w3_ref, deltas_ref,
                   hf_ref, hb_ref, *, tl, seq, c):
    i = pl.program_id(0)
    h = jnp.sin(f1_ref[...] * (_dot_3pass(z_ref[...], w1_ref[...]) + b1_ref[...]))
    h = jnp.sin(f2_ref[...] * (_dot_3pass(h, w2_ref[...]) + b2_ref[...]))
    h = _dot_3pass(h, w3_ref[...])
    pos = (lax.broadcasted_iota(jnp.int32, (tl, c), 0) + i * tl).astype(F32)
    decay = jnp.exp(-(pos * (1.0 / (seq - 1))) * deltas_ref[...]) + FILTER_SHIFT
    hf_ref[...] = h[:, :c] * decay
    hb_ref[...] = h[:, c:] * decay


def _filters(z, w1, b1, f1, w2, b2, f2, w3, deltas):
    L = z.shape[0]
    c = w3.shape[1] // 2
    tl = FILTER_TILE
    return pl.pallas_call(
        functools.partial(_filter_kernel, tl=tl, seq=L, c=c),
        grid=(L // tl,),
        in_specs=[pl.BlockSpec((tl, z.shape[1]), lambda i: (i, 0))]
        + [_full(a) for a in (w1, b1, f1, w2, b2, f2, w3, deltas)],
        out_specs=(pl.BlockSpec((tl, c), lambda i: (i, 0)), pl.BlockSpec((tl, c), lambda i: (i, 0))),
        out_shape=(jax.ShapeDtypeStruct((L, c), F32), jax.ShapeDtypeStruct((L, c), F32)),
        compiler_params=_params("parallel"),
        name="hyena_filter",
    )(z, w1, b1, f1, w2, b2, f2, w3, deltas)


def _dft_outer_kernel(u_ref, m_ref, ar_ref, ai_ref, *, n1):
    a = _dot(m_ref[...], u_ref[...].astype(BF16))
    ar_ref[...] = a[:n1]
    ai_ref[...] = a[n1:]


def _dft_outer(u2, m_outer):
    B, n1_in, W = u2.shape
    n1 = m_outer.shape[0] // 2
    lb = min(DFT_LANE_BLOCK, W)
    blk = lambda rows: pl.BlockSpec((None, rows, lb), lambda b, j: (b, 0, j))
    return pl.pallas_call(
        functools.partial(_dft_outer_kernel, n1=n1),
        grid=(B, W // lb),
        in_specs=[blk(n1_in), _full(m_outer)],
        out_specs=(blk(n1), blk(n1)),
        out_shape=(jax.ShapeDtypeStruct((B, n1, W), F32), jax.ShapeDtypeStruct((B, n1, W), F32)),
        compiler_params=_params("parallel", "parallel"),
        name="dft_outer",
    )(u2, m_outer)


def _twiddle(ar, ai, cph, sph, conj):
    if conj:
        return ar * cph - ai * sph, ai * cph + ar * sph
    return ar * cph + ai * sph, ai * cph - ar * sph


def _dft_spectrum_kernel(ar_ref, ai_ref, twc_ref, tws_ref, mf_ref, kf_ref, *, kb, n2, scale):
    for r in range(kb):
        pr, pi = _twiddle(ar_ref[r], ai_ref[r], twc_ref[r], tws_ref[r], conj=False)
        a = jnp.concatenate([pr.astype(BF16), pi.astype(BF16)], axis=0)
        kf_ref[r] = _dot(mf_ref[...], a) * scale


def _dft_spectrum(ar, ai, twc, tws, m_fwd, *, scale):
    n1, n2, c = ar.shape
    kb = DFT_K1_BLOCK
    blk = lambda rows, w: pl.BlockSpec((kb, rows, w), lambda i: (i, 0, 0))
    return pl.pallas_call(
        functools.partial(_dft_spectrum_kernel, kb=kb, n2=n2, scale=scale),
        grid=(n1 // kb,),
        in_specs=[blk(n2, c), blk(n2, c), blk(n2, 1), blk(n2, 1), _full(m_fwd)],
        out_specs=blk(2 * n2, c),
        out_shape=jax.ShapeDtypeStruct((n1, 2 * n2, c), F32),
        compiler_params=_params("parallel"),
        name="dft_filter_spectrum",
    )(ar, ai, twc, tws, m_fwd)


def _dft_inner_kernel(ar_ref, ai_ref, twc_ref, tws_ref, kf_ref, mf_ref, mi_ref, br_ref, bi_ref, *, kb, n2):
    for r in range(kb):
        cph = twc_ref[r]
        sph = tws_ref[r]
        pr, pi = _twiddle(ar_ref[r], ai_ref[r], cph, sph, conj=False)
        a = jnp.concatenate([pr.astype(BF16), pi.astype(BF16)], axis=0)
        x = _dot(mf_ref[...], a)
        xr, xi = x[:n2], x[n2:]
        kr, ki = kf_ref[r, :n2], kf_ref[r, n2:]
        y = jnp.concatenate([(xr * kr - xi * ki).astype(BF16), (xr * ki + xi * kr).astype(BF16)], axis=0)
        bb = _dot(mi_ref[...], y)
        qr, qi = _twiddle(bb[:n2], bb[n2:], cph, sph, conj=True)
        br_ref[r] = qr
        bi_ref[r] = qi


def _dft_inner(ar, ai, twc, tws, kf, m_fwd, m_inv):
    B, n1, n2, c = ar.shape
    kb = DFT_K1_BLOCK
    sig = pl.BlockSpec((None, kb, n2, c), lambda i, b: (b, i, 0, 0))
    tab = lambda rows, w: pl.BlockSpec((kb, rows, w), lambda i, b: (i, 0, 0))
    return pl.pallas_call(
        functools.partial(_dft_inner_kernel, kb=kb, n2=n2),
        grid=(n1 // kb, B),
        in_specs=[sig, sig, tab(n2, 1), tab(n2, 1), tab(2 * n2, c), _full(m_fwd), _full(m_inv)],
        out_specs=(sig, sig),
        out_shape=(jax.ShapeDtypeStruct(ar.shape, F32), jax.ShapeDtypeStruct(ar.shape, F32)),
        compiler_params=_params("parallel", "parallel"),
        name="dft_inner",
    )(ar, ai, twc, tws, kf, m_fwd, m_inv)


def _dft_outer_inv_kernel(br_ref, bi_ref, m_ref, u_ref, x0_ref, d_ref, o_ref):
    b = jnp.concatenate([br_ref[...].astype(BF16), bi_ref[...].astype(BF16)], axis=0)
    y = _dot(m_ref[...], b)
    u = u_ref[...]
    o_ref[...] = (x0_ref[...] * (y + u * d_ref[...])).astype(o_ref.dtype)


def _dft_outer_inv(br, bi, m_inv_outer, u2, x02, d_lanes):
    B, n1, W = br.shape
    lb = d_lanes.shape[1]
    blk = lambda rows: pl.BlockSpec((None, rows, lb), lambda b, j: (b, 0, j))
    return pl.pallas_call(
        _dft_outer_inv_kernel,
        grid=(B, W // lb),
        in_specs=[blk(n1), blk(n1), _full(m_inv_outer), blk(n1 // 2), blk(n1 // 2), _full(d_lanes)],
        out_specs=blk(n1 // 2),
        out_shape=jax.ShapeDtypeStruct((B, n1 // 2, W), BF16),
        compiler_params=_params("parallel", "parallel"),
        name="dft_outer_inv_gate",
    )(br, bi, m_inv_outer, u2, x02, d_lanes)


@functools.lru_cache(maxsize=None)
def _dft_tables(seq):
    n = 2 * seq
    n2 = DFT_N2
    n1 = n // n2
    k1 = np.arange(n1)[:, None]
    th = 2.0 * np.pi * ((k1 * np.arange(n1)[None, :]) % n1) / n1
    c1, s1 = np.cos(th), np.sin(th)
    m_outer_full = np.concatenate([c1, -s1], axis=0)
    m_outer_half = m_outer_full[:, :n1 // 2]
    m_outer_inv = np.concatenate([c1[:n1 // 2], -s1[:n1 // 2]], axis=1)
    ps = 2.0 * np.pi * ((np.arange(n2)[:, None] * np.arange(n2)[None, :]) % n2) / n2
    c2, s2 = np.cos(ps), np.sin(ps)
    m_fwd = np.block([[c2, s2], [-s2, c2]])
    m_inv = np.block([[c2, -s2], [s2, c2]])
    ph = 2.0 * np.pi * (k1 * np.arange(n2)[None, :]) / n
    f = lambda a: np.asarray(a, np.float32)
    return dict(n1=n1, n2=n2, m_outer_full=f(m_outer_full), m_outer_half=f(m_outer_half),
                m_outer_inv=f(m_outer_inv), m_fwd=f(m_fwd), m_inv=f(m_inv),
                twc=f(np.cos(ph))[:, :, None], tws=f(np.sin(ph))[:, :, None])


def _filter_features(seq):
    t = jnp.linspace(0.0, 1.0, seq, dtype=F32)[:, None]
    w = 2.0 * math.pi * jnp.arange(seq, dtype=F32) / seq
    f = jnp.linspace(1e-4, FILTER_BANDS - 1, FILTER_BANDS, dtype=F32)
    ang = w[:, None] * f[None, :]
    z = jnp.concatenate([t, jnp.cos(ang), -jnp.sin(ang)], axis=-1)
    return jnp.pad(z, ((0, 0), (0, LANES - FILTER_EMB)))


def _hyena(phy, lp):
    B, L, W = phy.shape
    c = W // 3
    tb = _dft_tables(L)
    n1, n2 = tb["n1"], tb["n2"]
    bf = lambda name: jnp.asarray(tb[name]).astype(BF16)
    twc, tws = jnp.asarray(tb["twc"]), jnp.asarray(tb["tws"])

    u, x0 = _hygate(phy, lp["conv_w"], lp["conv_b"])

    hf, hb = _filters(_filter_features(L), lp["filt_w1"], lp["filt_b1"], lp["filt_f1"], lp["filt_w2"],
                      lp["filt_b2"], lp["filt_f2"], lp["filt_w3"], lp["deltas"])
    kern = jnp.concatenate([hf, jnp.zeros((1, c), F32), hb[:0:-1]], axis=0)
    far, fai = _dft_outer(kern.reshape(1, n1, n2 * c), bf("m_outer_full"))
    kf = _dft_spectrum(far.reshape(n1, n2, c), fai.reshape(n1, n2, c), twc, tws, bf("m_fwd"),
                       scale=1.0 / (2 * L))

    u2 = u.reshape(B, n1 // 2, n2 * c)
    ar, ai = _dft_outer(u2, bf("m_outer_half"))
    br, bi = _dft_inner(ar.reshape(B, n1, n2, c), ai.reshape(B, n1, n2, c), twc, tws, kf,
                        bf("m_fwd"), bf("m_inv"))
    lb = min(DFT_LANE_BLOCK, n2 * c)
    d_lanes = jnp.tile(lp["hyena_d"], (1, lb // c))
    hyo = _dft_outer_inv(br.reshape(B, n1, n2 * c), bi.reshape(B, n1, n2 * c), bf("m_outer_inv"),
                         u2, x0.reshape(B, n1 // 2, n2 * c), d_lanes)
    return hyo.reshape(B, L, c)


def _outproj_kernel(att_ref, hyo_ref, x_ref, wa_ref, wh_ref, bo_ref, g_ref, b_ref, wr_hi_ref, wr_lo_ref,
                    br_ref, x1_ref, comb_ref, *, alpha):
    m = _dot(att_ref[...], wa_ref[...]) + _dot(hyo_ref[...], wh_ref[...]) + bo_ref[...]
    x1 = _layer_norm(alpha * x_ref[...] + m, g_ref[...], b_ref[...])
    x1_ref[...] = x1

    xh, xl = _split_bf16(x1)
    lg = _dot(xh, wr_hi_ref[...]) + (_dot(xh, wr_lo_ref[...]) + _dot(xl, wr_hi_ref[...])) + br_ref[...]
    lane = lax.broadcasted_iota(jnp.int32, lg.shape, 1)
    big = jnp.int32(ROUTER_LANES)
    ninf = jnp.float32(-jnp.inf)
    first = lambda mask: jnp.min(jnp.where(mask, lane, big), axis=-1, keepdims=True)

    gmask = lane < N_GROUPS
    gmax = jnp.max(jnp.where(gmask, lg, ninf), axis=-1, keepdims=True)
    gsum = jnp.sum(jnp.where(gmask, jnp.exp(lg - gmax), 0.0), axis=-1, keepdims=True)
    g_w = 1.0 / gsum
    g_idx = first(gmask & (lg == gmax))
    lo = N_GROUPS + EXPERTS_PER_GROUP * g_idx
    emask = (lane >= lo) & (lane < lo + EXPERTS_PER_GROUP)
    e1 = jnp.max(jnp.where(emask, lg, ninf), axis=-1, keepdims=True)
    i1 = first(emask & (lg == e1))
    rest = emask & (lane != i1)
    e2 = jnp.max(jnp.where(rest, lg, ninf), axis=-1, keepdims=True)
    i2 = first(rest & (lg == e2))
    tt = jnp.exp(e2 - e1)
    w1 = g_w / (1.0 + tt)
    comb_ref[...] = (jnp.where(lane == i1, w1, 0.0) + jnp.where(lane == i2, w1 * tt, 0.0)
                     + jnp.where(lane == 0, g_idx.astype(F32), 0.0))


def _outproj_ln_router(att, hyo, x, wa, wh, bo, g, b, wr_hi, wr_lo, br, *, alpha):
    B, L, D = x.shape
    tm = PROJ_TILE
    tok = lambda w: pl.BlockSpec((None, tm, w), lambda bb, i: (bb, i, 0))
    return pl.pallas_call(
        functools.partial(_outproj_kernel, alpha=alpha),
        grid=(B, L // tm),
        in_specs=[tok(att.shape[2]), tok(hyo.shape[2]), tok(D)]
        + [_full(a) for a in (wa, wh, bo, g, b, wr_hi, wr_lo, br)],
        out_specs=(tok(D), tok(ROUTER_LANES)),
        out_shape=(jax.ShapeDtypeStruct((B, L, D), F32), jax.ShapeDtypeStruct((B, L, ROUTER_LANES), F32)),
        compiler_params=_params("parallel", "parallel"),
        name="outproj_ln_router",
    )(att, hyo, x, wa, wh, bo, g, b, wr_hi, wr_lo, br)


def _moe_kernel(meta_ref, x_ref, comb_ref, w1_ref, w3_ref, w2_ref, g_ref, b_ref, o_ref,
                xs_ref, ys_ref, cs_ref, pt_ref, *, alpha, tb, slots, tile):
    bi = pl.program_id(0)
    blk = pl.program_id(1)
    e = pl.program_id(2)
    grp = e // EXPERTS_PER_GROUP

    @pl.when(e == 0)
    def _():
        comb = comb_ref[...]
        starts = [meta_ref[bi, blk, g].astype(F32) for g in range(N_GROUPS)]
        gid_row = comb.T[0:1, :]
        grow = lax.broadcasted_iota(jnp.int32, (8, tb), 0).astype(F32)
        oh_row = jnp.where(gid_row == grow, 1.0, 0.0)
        r_i = lax.broadcasted_iota(jnp.int32, (tb, tb), 0)
        c_i = lax.broadcasted_iota(jnp.int32, (tb, tb), 1)
        earlier_row = jnp.where(r_i < c_i, 1.0, 0.0).astype(BF16)
        rank_row = jnp.sum(oh_row * _dot(oh_row.astype(BF16), earlier_row), axis=0, keepdims=True)
        pos_row = rank_row
        for g in range(N_GROUPS):
            pos_row = pos_row + oh_row[g:g + 1] * starts[g]
        slot_r = lax.broadcasted_iota(jnp.int32, (slots, tb), 0).astype(F32)
        p = jnp.where(slot_r == pos_row, 1.0, 0.0).astype(BF16)
        xs_ref[...] = _dot(p, x_ref[...].astype(BF16)).astype(BF16)
        ch, cl = _split_bf16(comb)
        cs_ref[...] = _dot(p, ch) + _dot(p, cl)

        gid_col = comb[:, 0:1]
        glane = lax.broadcasted_iota(jnp.int32, comb.shape, 1).astype(F32)
        oh_col = jnp.where(gid_col == glane, 1.0, 0.0)
        earlier_col = jnp.where(c_i < r_i, 1.0, 0.0).astype(BF16)
        rank_col = jnp.sum(oh_col * _dot(earlier_col, oh_col.astype(BF16)), axis=1, keepdims=True)
        pos_col = rank_col
        for g in range(N_GROUPS):
            pos_col = pos_col + oh_col[:, g:g + 1] * starts[g]
        slot_c = lax.broadcasted_iota(jnp.int32, (tb, slots), 1).astype(F32)
        pt_ref[...] = jnp.where(slot_c == pos_col, 1.0, 0.0).astype(BF16)
        ys_ref[...] = jnp.zeros_like(ys_ref)

    start = meta_ref[bi, blk, grp]
    n_tiles = meta_ref[bi, blk, N_GROUPS + grp]

    def row_tile(i, carry):
        rows = pl.ds(pl.multiple_of(start + i * tile, MOE_GROUP_ALIGN), tile)
        xt = xs_ref[rows, :]
        hid = jax.nn.silu(_dot(xt, w1_ref[...])) * _dot(xt, w3_ref[...])
        y = _dot(hid.astype(BF16), w2_ref[...])
        c = cs_ref[rows, :]
        lane = lax.broadcasted_iota(jnp.int32, c.shape, 1)
        ce = jnp.sum(jnp.where(lane == N_GROUPS + e, c, 0.0), axis=-1, keepdims=True)
        ys_ref[rows, :] += ce * y
        return carry

    lax.fori_loop(0, n_tiles, row_tile, 0)

    @pl.when(e == pl.num_programs(2) - 1)
    def _():
        f = _dot(pt_ref[...], ys_ref[...].astype(BF16))
        o_ref[...] = _layer_norm(alpha * x_ref[...] + f, g_ref[...], b_ref[...])


def _moe_ln(x1, comb, w1, w3, w2, g, b, *, alpha):
    B, L, D = x1.shape
    E, _, de = w1.shape
    tb = min(MOE_BLOCK, L)
    tile = MOE_ROW_TILE
    nblk = L // tb
    slots = tb + 2 * tile
    gid = comb[..., 0].astype(jnp.int32).reshape(B, nblk, tb)
    cnt = jnp.sum(gid[..., None] == jnp.arange(N_GROUPS), axis=2).astype(jnp.int32)
    padded = (cnt + MOE_GROUP_ALIGN - 1) // MOE_GROUP_ALIGN * MOE_GROUP_ALIGN
    meta = jnp.concatenate([jnp.cumsum(padded, axis=-1) - padded, (cnt + tile - 1) // tile], axis=-1)

    tok = lambda w: pl.BlockSpec((None, tb, w), lambda bb, i, e, m: (bb, i, 0))
    const = lambda a: pl.BlockSpec(a.shape, lambda bb, i, e, m: (0,) * a.ndim)
    return pl.pallas_call(
        functools.partial(_moe_kernel, alpha=alpha, tb=tb, slots=slots, tile=tile),
        grid_spec=pltpu.PrefetchScalarGridSpec(
            num_scalar_prefetch=1,
            grid=(B, nblk, E),
            in_specs=[tok(D), tok(ROUTER_LANES),
                      pl.BlockSpec((None, D, de), lambda bb, i, e, m: (e, 0, 0)),
                      pl.BlockSpec((None, D, de), lambda bb, i, e, m: (e, 0, 0)),
                      pl.BlockSpec((None, de, D), lambda bb, i, e, m: (e, 0, 0)),
                      const(g), const(b)],
            out_specs=tok(D),
            scratch_shapes=[pltpu.VMEM((slots, D), BF16), pltpu.VMEM((slots, D), F32),
                            pltpu.VMEM((slots, ROUTER_LANES), F32), pltpu.VMEM((tb, slots), BF16)]),
        out_shape=jax.ShapeDtypeStruct((B, L, D), F32),
        compiler_params=_params("parallel", "parallel", "arbitrary"),
        name="moe_ln",
    )(meta, x1, comb, w1, w3, w2, g, b)


def _alibi_tables():
    t = ATT_TILE
    slopes = jnp.asarray(2.0 ** (-8.0 * np.arange(1, ATT_HEADS + 1) / ATT_HEADS), dtype=F32)
    off = (jnp.arange(t)[:, None] - jnp.arange(t)[None, :]).astype(F32)
    dpos = slopes[:, None, None] * off[None]
    return slopes, jnp.stack([dpos, -jnp.abs(dpos), -dpos], axis=1)


def _prep_layer(l, depth, w_in, b_in, conv_w, conv_b, lam_q1, lam_k1, lam_q2, lam_k2, subln_g,
                filt_w1, filt_b1, filt_freq1, filt_w2, filt_b2, filt_freq2, filt_w3, hyena_d,
                w_out, b_out, ln1_g, ln1_b, router_group_w, router_group_b, router_expert_w,
                router_expert_b, exp_w1, exp_w3, exp_w2, ln2_g, ln2_b):
    a0, a1, a2 = ATT_QK_WIDTH, 2 * ATT_QK_WIDTH, 2 * ATT_QK_WIDTH + ATT_WIDTH
    scale = ATT_HEAD_DIM ** -0.5
    row = lambda v: v.reshape(1, -1)
    w, b = w_in[l], b_in[l]
    c = hyena_d.shape[1]
    wr = jnp.concatenate([router_group_w[l], router_expert_w[l]], axis=1)
    wr = jnp.pad(wr, ((0, 0), (0, ROUTER_LANES - wr.shape[1])))
    wr_hi = wr.astype(BF16)
    br = jnp.concatenate([router_group_b[l], router_expert_b[l]])
    max_decay = abs(math.log(DECAY_TARGET) / FAST_DECAY_PCT)
    min_decay = abs(math.log(DECAY_TARGET) / SLOW_DECAY_PCT)
    return dict(
        lam_init=0.8 - 0.6 * math.exp(-0.3 * l),
        alpha=(2.0 * depth) ** 0.25,
        wq=(w[:, :a0] * scale).astype(BF16), bq=row(b[:a0] * scale),
        wk=w[:, a0:a1].astype(BF16), bk=row(b[a0:a1]),
        wvt=w[:, a1:a2].T.astype(BF16), bvt=b[a1:a2].reshape(-1, 1),
        why=w[:, a2:].astype(BF16), bhy=row(b[a2:]),
        conv_w=conv_w[l], conv_b=row(conv_b[l]),
        lamv=jnp.stack([lam_q1[l], lam_k1[l], lam_q2[l], lam_k2[l]]).astype(F32),
        subln_g=row(subln_g[l]),
        filt_w1=jnp.pad(filt_w1[l], ((0, LANES - FILTER_EMB), (0, 0))), filt_b1=row(filt_b1[l]),
        filt_f1=row(filt_freq1[l]), filt_w2=filt_w2[l], filt_b2=row(filt_b2[l]), filt_f2=row(filt_freq2[l]),
        filt_w3=filt_w3[l],
        deltas=row(jnp.linspace(min_decay, max_decay, c, dtype=F32)),
        hyena_d=row(hyena_d[l]),
        wa=w_out[l][:ATT_WIDTH].astype(BF16), wh=w_out[l][ATT_WIDTH:].astype(BF16), bo=row(b_out[l]),
        ln1_g=row(ln1_g[l]), ln1_b=row(ln1_b[l]),
        wr_hi=wr_hi, wr_lo=(wr - wr_hi.astype(F32)).astype(BF16),
        br=row(jnp.pad(br, (0, ROUTER_LANES - br.shape[0]))),
        w1=exp_w1[l].astype(BF16), w3=exp_w3[l].astype(BF16), w2=exp_w2[l].astype(BF16),
        ln2_g=row(ln2_g[l]), ln2_b=row(ln2_b[l]),
    )


def _layer(x, lp, alibi):
    slopes, dist = alibi
    q, k, vt, phy = _inproj(x, lp["wq"], lp["bq"], lp["wk"], lp["bk"], lp["wvt"], lp["bvt"],
                            lp["why"], lp["bhy"])
    att = _attention(q, k, vt, slopes, dist, lp["lamv"], lp["subln_g"],
                     lam_init=lp["lam_init"])
    hyo = _hyena(phy, lp)
    x1, comb = _outproj_ln_router(att, hyo, x, lp["wa"], lp["wh"], lp["bo"], lp["ln1_g"], lp["ln1_b"],
                                  lp["wr_hi"], lp["wr_lo"], lp["br"], alpha=lp["alpha"])
    return _moe_ln(x1, comb, lp["w1"], lp["w3"], lp["w2"], lp["ln2_g"], lp["ln2_b"], alpha=lp["alpha"])


def kernel(x_prompt, x_sample, w_in, b_in, conv_w, conv_b, lam_q1, lam_k1, lam_q2, lam_k2, subln_g, filt_w1, filt_b1, filt_freq1, filt_w2, filt_b2, filt_freq2, filt_w3, hyena_d, w_out, b_out, ln1_g, ln1_b, router_group_w, router_group_b, router_expert_w, router_expert_b, exp_w1, exp_w3, exp_w2, ln2_g, ln2_b):
    params = (w_in, b_in, conv_w, conv_b, lam_q1, lam_k1, lam_q2, lam_k2, subln_g, filt_w1, filt_b1,
              filt_freq1, filt_w2, filt_b2, filt_freq2, filt_w3, hyena_d, w_out, b_out, ln1_g, ln1_b,
              router_group_w, router_group_b, router_expert_w, router_expert_b, exp_w1, exp_w3, exp_w2,
              ln2_g, ln2_b)
    depth = w_in.shape[0]
    layers = [_prep_layer(l, depth, *params) for l in range(depth)]
    alibi = _alibi_tables()

    def trunk(x):
        for lp in layers:
            x = _layer(x, lp, alibi)
        return x

    return (trunk(x_prompt), trunk(x_sample))
```

```python
import functools
import math

import numpy as np
import jax
import jax.numpy as jnp
from jax import lax
from jax.experimental import pallas as pl
from jax.experimental.pallas import tpu as pltpu

F32 = jnp.float32
BF16 = jnp.bfloat16

ATT_HEADS = 4
ATT_HEAD_DIM = 64
ATT_V_DIM = 2 * ATT_HEAD_DIM
ATT_QK_WIDTH = ATT_HEADS * 2 * ATT_HEAD_DIM
ATT_WIDTH = ATT_HEADS * ATT_V_DIM
FILTER_EMB = 33
FILTER_BANDS = (FILTER_EMB - 1) // 2
DECAY_TARGET = 1e-2
FAST_DECAY_PCT = 0.3
SLOW_DECAY_PCT = 1.5
FILTER_SHIFT = 0.05
N_GROUPS = 4
EXPERTS_PER_GROUP = 4
N_EXPERTS = N_GROUPS * EXPERTS_PER_GROUP
LN_EPS = 1e-5
RMS_EPS = 1e-5

LANES = 128
BF16_SUBLANES = 16
VMEM_LIMIT_BYTES = 56 * 1024 * 1024

ATT_TILE = 256
ATT_KEY_BLOCK = 1024
VT_ROWS = ATT_V_DIM + BF16_SUBLANES
PROJ_TILE = 512
GATE_TILE = 512
FILTER_TILE = 512
DFT_N2 = 128
DFT_LANE_BLOCK = 2048
DFT_K1_BLOCK = 8
MOE_BLOCK = 1024
MOE_ROW_TILE = 128
MOE_GROUP_ALIGN = BF16_SUBLANES
ROUTER_LANES = LANES
NEG_BIG = -1e30
LOG2E = math.log2(math.e)


def _params(*sem):
    return pltpu.CompilerParams(dimension_semantics=sem, vmem_limit_bytes=VMEM_LIMIT_BYTES)


def _full(a):
    nd = a.ndim
    return pl.BlockSpec(a.shape, lambda *_: (0,) * nd)


def _dot(a, b):
    return jnp.dot(a, b, preferred_element_type=F32)


def _dot_nt(a, b):
    return lax.dot_general(a, b, (((1,), (1,)), ((), ())), preferred_element_type=F32)


def _split_bf16(x):
    hi = x.astype(BF16)
    lo = (x - hi.astype(F32)).astype(BF16)
    return hi, lo


def _dot_3pass(a, b):
    ah, al = _split_bf16(a)
    bh, bl = _split_bf16(b)
    return _dot(ah, bh) + (_dot(ah, bl) + _dot(al, bh))


def _layer_norm(r, g, b):
    mu = jnp.mean(r, axis=-1, keepdims=True)
    c = r - mu
    var = jnp.mean(c * c, axis=-1, keepdims=True)
    return c * lax.rsqrt(var + LN_EPS) * g + b


def _inproj_kernel(x_ref, wq_ref, bq_ref, wk_ref, bk_ref, wvt_ref, bvt_ref, why_ref, bhy_ref,
                   q_ref, k_ref, vt_ref, phy_ref):
    xb = x_ref[...].astype(BF16)
    tm = xb.shape[0]
    q_ref[...] = (_dot(xb, wq_ref[...]) + bq_ref[...]).astype(BF16)
    k_ref[...] = (_dot(xb, wk_ref[...]) + bk_ref[...]).astype(BF16)
    vt = (_dot_nt(wvt_ref[...], xb) + bvt_ref[...]).astype(BF16)
    row = lax.broadcasted_iota(jnp.int32, (BF16_SUBLANES, tm), 0)
    ones_rows = jnp.where(row == 0, 1.0, 0.0).astype(BF16)
    for h in range(ATT_HEADS):
        vt_ref[h, :ATT_V_DIM, :] = vt[h * ATT_V_DIM:(h + 1) * ATT_V_DIM]
        vt_ref[h, ATT_V_DIM:, :] = ones_rows
    phy_ref[...] = (_dot(xb, why_ref[...]) + bhy_ref[...]).astype(phy_ref.dtype)


def _inproj(x, wq, bq, wk, bk, wvt, bvt, why, bhy):
    B, L, D = x.shape
    tm, kb = PROJ_TILE, min(ATT_KEY_BLOCK, L)
    per = kb // tm
    hyw = why.shape[1]
    tok = lambda w: pl.BlockSpec((None, tm, w), lambda b, i: (b, i, 0))
    return pl.pallas_call(
        _inproj_kernel,
        grid=(B, L // tm),
        in_specs=[tok(D)] + [_full(a) for a in (wq, bq, wk, bk, wvt, bvt, why, bhy)],
        out_specs=(tok(ATT_QK_WIDTH), tok(ATT_QK_WIDTH),
                   pl.BlockSpec((None, None, ATT_HEADS, VT_ROWS, tm),
                                lambda b, i: (b, i // per, 0, 0, i % per)),
                   tok(hyw)),
        out_shape=(jax.ShapeDtypeStruct((B, L, ATT_QK_WIDTH), BF16),
                   jax.ShapeDtypeStruct((B, L, ATT_QK_WIDTH), BF16),
                   jax.ShapeDtypeStruct((B, L // kb, ATT_HEADS, VT_ROWS, kb), BF16),
                   jax.ShapeDtypeStruct((B, L, hyw), BF16)),
        compiler_params=_params("parallel", "parallel"),
        name="inproj",
    )(x, wq, bq, wk, bk, wvt, bvt, why, bhy)


def _attn_kernel(slopes_ref, q_ref, k_ref, vt_ref, dist_ref, lam_ref, g_ref,
                 o_ref, acc_ref, s_ref, *, t, kb, nkb, lam_init):
    h = pl.program_id(1)
    qi = pl.program_id(2)
    ns = kb // t
    slope = slopes_ref[h]
    q = q_ref[...]
    lane = lax.broadcasted_iota(jnp.int32, q.shape, 1)
    zero = jnp.zeros_like(q)
    qmaps = (jnp.where(lane < ATT_HEAD_DIM, q, zero), jnp.where(lane >= ATT_HEAD_DIM, q, zero))
    acc_ref[...] = jnp.zeros_like(acc_ref)

    def produce(j, slot):
        kblk = k_ref[pl.ds(pl.multiple_of(j * kb, kb), kb), :]
        for mi in range(2):
            s = _dot_nt(kblk, qmaps[mi])
            for st in range(ns):
                sel = jnp.clip(j * ns + st - qi, -1, 1) + 1
                s_ref[slot, mi, st * t:(st + 1) * t, :] = s[st * t:(st + 1) * t] + dist_ref[sel]

    def consume(j, slot, ms):
        vt = vt_ref[j]
        cs = [-slope * (jnp.abs(j * ns + st - qi) * t).astype(F32) for st in range(ns)]
        out = []
        for mi in range(2):
            sub = [s_ref[slot, mi, st * t:(st + 1) * t, :] for st in range(ns)]
            m_new = ms[mi]
            for st in range(ns):
                m_new = jnp.maximum(m_new, jnp.max(sub[st], axis=0, keepdims=True) + cs[st])
            e = jnp.concatenate([jnp.exp2(sub[st] - (m_new - cs[st])).astype(BF16) for st in range(ns)],
                                axis=0)
            acc_ref[mi] = acc_ref[mi] * jnp.exp2(ms[mi] - m_new) + _dot(vt, e)
            out.append(m_new)
        return tuple(out)

    m0 = jnp.full((1, t), NEG_BIG, F32)
    ms = (m0, m0)
    produce(0, 0)
    if nkb > 1:
        def pair(i, ms):
            produce(2 * i + 1, 1)
            ms = consume(2 * i, 0, ms)
            produce(2 * i + 2, 0)
            return consume(2 * i + 1, 1, ms)

        ms = lax.fori_loop(0, nkb // 2 - 1, pair, ms)
        produce(nkb - 1, 1)
        ms = consume(nkb - 2, 0, ms)
        consume(nkb - 1, 1, ms)
    else:
        consume(0, 0, ms)

    a0 = acc_ref[0]
    a1 = acc_ref[1]
    o0 = a0[:ATT_V_DIM] / a0[ATT_V_DIM:ATT_V_DIM + 1]
    o1 = a1[:ATT_V_DIM] / a1[ATT_V_DIM:ATT_V_DIM + 1]
    lv = lam_ref[...]
    lam = (jnp.exp(jnp.sum(lv[0:1] * lv[1:2], axis=-1, keepdims=True))
           - jnp.exp(jnp.sum(lv[2:3] * lv[3:4], axis=-1, keepdims=True)) + lam_init)
    d = o0 - lam * o1
    d = d * lax.rsqrt(jnp.mean(d * d, axis=0, keepdims=True) + RMS_EPS)
    o_ref[...] = (d.T * (g_ref[...] * (1.0 - lam_init))).astype(o_ref.dtype)


def _attention(q, k, vt, slopes, dist, lamv, g, *, lam_init):
    B, L, _ = q.shape
    t = ATT_TILE
    nkb, kb = vt.shape[1], vt.shape[4]
    assert nkb == 1 or nkb % 2 == 0
    return pl.pallas_call(
        functools.partial(_attn_kernel, t=t, kb=kb, nkb=nkb, lam_init=lam_init),
        grid=(B, ATT_HEADS, L // t),
        in_specs=[pl.BlockSpec(memory_space=pltpu.SMEM),
                  pl.BlockSpec((None, t, LANES), lambda b, h, i: (b, i, h)),
                  pl.BlockSpec((None, L, LANES), lambda b, h, i: (b, 0, h)),
                  pl.BlockSpec((None, nkb, None, VT_ROWS, kb), lambda b, h, i: (b, 0, h, 0, 0)),
                  pl.BlockSpec((None, 3, t, t), lambda b, h, i: (h, 0, 0, 0)),
                  _full(lamv), _full(g)],
        out_specs=pl.BlockSpec((None, t, LANES), lambda b, h, i: (b, i, h)),
        out_shape=jax.ShapeDtypeStruct((B, L, ATT_WIDTH), BF16),
        scratch_shapes=[pltpu.VMEM((2, VT_ROWS, t), F32), pltpu.VMEM((2, 2, kb, t), F32)],
        compiler_params=_params("parallel", "parallel", "arbitrary"),
        name="diff_attention",
    )(slopes, q, k, vt, dist, lamv, g)


def _hygate_kernel(p_ref, prev_ref, next_ref, w_ref, b_ref, u_ref, x0_ref, *, tl, c):
    i = pl.program_id(1)
    last = pl.num_programs(1) - 1
    x = p_ref[...].astype(F32)
    sub = prev_ref.shape[0]
    prev_row = jnp.where(i == 0, 0.0, prev_ref[...].astype(F32)[sub - 1:sub, :])
    next_row = jnp.where(i == last, 0.0, next_ref[...].astype(F32)[0:1, :])
    row = lax.broadcasted_iota(jnp.int32, x.shape, 0)
    xm = jnp.where(row == 0, prev_row, pltpu.roll(x, 1, 0))
    xp = jnp.where(row == tl - 1, next_row, pltpu.roll(x, tl - 1, 0))
    w = w_ref[...]
    hy = xm * w[0:1] + x * w[1:2] + xp * w[2:3] + b_ref[...]
    x0_ref[...] = hy[:, :c].astype(x0_ref.dtype)
    u_ref[...] = (hy[:, 2 * c:] * hy[:, c:2 * c]).astype(u_ref.dtype)


def _hygate(phy, conv_w, conv_b):
    B, L, W = phy.shape
    c = W // 3
    tl = GATE_TILE
    sub = BF16_SUBLANES
    nb = tl // sub
    last_blk = L // sub - 1
    return pl.pallas_call(
        functools.partial(_hygate_kernel, tl=tl, c=c),
        grid=(B, L // tl),
        in_specs=[pl.BlockSpec((None, tl, W), lambda b, i: (b, i, 0)),
                  pl.BlockSpec((None, sub, W), lambda b, i: (b, jnp.maximum(i * nb - 1, 0), 0)),
                  pl.BlockSpec((None, sub, W), lambda b, i: (b, jnp.minimum((i + 1) * nb, last_blk), 0)),
                  _full(conv_w), _full(conv_b)],
        out_specs=(pl.BlockSpec((None, tl, c), lambda b, i: (b, i, 0)),
                   pl.BlockSpec((None, tl, c), lambda b, i: (b, i, 0))),
        out_shape=(jax.ShapeDtypeStruct((B, L, c), BF16), jax.ShapeDtypeStruct((B, L, c), BF16)),
        compiler_params=_params("parallel", "parallel"),
        name="hyena_gate",
    )(phy, phy, phy, conv_w, conv_b)


def _filter_kernel(z_ref, w1_ref, b1_ref, f1_ref, w2_ref, b2_ref, f2_ref, w3_ref, deltas_ref,
                   k_ref, *, tl, seq, c):
    i = pl.program_id(0)
    h = jnp.sin(f1_ref[...] * (_dot_3pass(z_ref[...], w1_ref[...]) + b1_ref[...]))
    h = jnp.sin(f2_ref[...] * (_dot_3pass(h, w2_ref[...]) + b2_ref[...]))
    h = _dot_3pass(h, w3_ref[...])
    n = lax.broadcasted_iota(jnp.int32, (tl, c), 0) + i * tl
    lag = jnp.where(n < seq, n, 2 * seq - n).astype(F32)
    decay = jnp.exp(-(lag * (1.0 / (seq - 1))) * deltas_ref[...]) + FILTER_SHIFT
    k_ref[...] = jnp.where(n == seq, 0.0, h * decay)


def _filters(z2, w1, b1, f1, w2, b2, f2, w3, deltas):
    n = z2.shape[0]
    L = n // 2
    c = w3.shape[1] // 2
    tl = FILTER_TILE
    half = L // tl
    return pl.pallas_call(
        functools.partial(_filter_kernel, tl=tl, seq=L, c=c),
        grid=(n // tl,),
        in_specs=[pl.BlockSpec((tl, z2.shape[1]), lambda i: (i, 0))]
        + [_full(a) for a in (w1, b1, f1, w2, b2, f2)]
        + [pl.BlockSpec((w3.shape[0], c), lambda i: (0, i // half)), _full(deltas)],
        out_specs=pl.BlockSpec((tl, c), lambda i: (i, 0)),
        out_shape=jax.ShapeDtypeStruct((n, c), F32),
        compiler_params=_params("parallel"),
        name="hyena_filter",
    )(z2, w1, b1, f1, w2, b2, f2, w3, deltas)


def _dft_outer_kernel(u_ref, m_ref, ar_ref, ai_ref, *, n1):
    a = _dot(m_ref[...], u_ref[...].astype(BF16))
    ar_ref[...] = a[:n1].astype(ar_ref.dtype)
    ai_ref[...] = a[n1:].astype(ai_ref.dtype)


def _dft_outer(u2, m_outer, out_dtype):
    B, n1_in, W = u2.shape
    n1 = m_outer.shape[0] // 2
    lb = min(DFT_LANE_BLOCK, W)
    blk = lambda rows: pl.BlockSpec((None, rows, lb), lambda b, j: (b, 0, j))
    return pl.pallas_call(
        functools.partial(_dft_outer_kernel, n1=n1),
        grid=(B, W // lb),
        in_specs=[blk(n1_in), _full(m_outer)],
        out_specs=(blk(n1), blk(n1)),
        out_shape=(jax.ShapeDtypeStruct((B, n1, W), out_dtype), jax.ShapeDtypeStruct((B, n1, W), out_dtype)),
        compiler_params=_params("parallel", "parallel"),
        name="dft_outer",
    )(u2, m_outer)


def _twiddle(ar, ai, cph, sph, conj):
    if conj:
        return ar * cph - ai * sph, ai * cph + ar * sph
    return ar * cph + ai * sph, ai * cph - ar * sph


def _dft_spectrum_kernel(ar_ref, ai_ref, twc_ref, tws_ref, mf_ref, kf_ref, *, kb, n2, scale):
    for r in range(kb):
        pr, pi = _twiddle(ar_ref[r], ai_ref[r], twc_ref[r], tws_ref[r], conj=False)
        a = jnp.concatenate([pr.astype(BF16), pi.astype(BF16)], axis=0)
        kf_ref[r] = _dot(mf_ref[...], a) * scale


def _dft_spectrum(ar, ai, twc, tws, m_fwd, *, scale):
    n1, n2, c = ar.shape
    kb = DFT_K1_BLOCK
    blk = lambda rows, w: pl.BlockSpec((kb, rows, w), lambda i: (i, 0, 0))
    return pl.pallas_call(
        functools.partial(_dft_spectrum_kernel, kb=kb, n2=n2, scale=scale),
        grid=(n1 // kb,),
        in_specs=[blk(n2, c), blk(n2, c), blk(n2, 1), blk(n2, 1), _full(m_fwd)],
        out_specs=blk(2 * n2, c),
        out_shape=jax.ShapeDtypeStruct((n1, 2 * n2, c), F32),
        compiler_params=_params("parallel"),
        name="dft_filter_spectrum",
    )(ar, ai, twc, tws, m_fwd)


def _dft_inner_kernel(ar_ref, ai_ref, twc_ref, tws_ref, kf_ref, mf_ref, mi_ref, br_ref, bi_ref, *, kb, n2):
    for r in range(kb):
        cph = twc_ref[r]
        sph = tws_ref[r]
        pr, pi = _twiddle(ar_ref[r].astype(F32), ai_ref[r].astype(F32), cph, sph, conj=False)
        a = jnp.concatenate([pr.astype(BF16), pi.astype(BF16)], axis=0)
        x = _dot(mf_ref[...], a)
        xr, xi = x[:n2], x[n2:]
        kr, ki = kf_ref[r, :n2], kf_ref[r, n2:]
        y = jnp.concatenate([(xr * kr - xi * ki).astype(BF16), (xr * ki + xi * kr).astype(BF16)], axis=0)
        bb = _dot(mi_ref[...], y)
        qr, qi = _twiddle(bb[:n2], bb[n2:], cph, sph, conj=True)
        br_ref[r] = qr.astype(br_ref.dtype)
        bi_ref[r] = qi.astype(bi_ref.dtype)


def _dft_inner(ar, ai, twc, tws, kf, m_fwd, m_inv):
    B, n1, n2, c = ar.shape
    kb = DFT_K1_BLOCK
    sig = pl.BlockSpec((None, kb, n2, c), lambda i, b: (b, i, 0, 0))
    tab = lambda rows, w: pl.BlockSpec((kb, rows, w), lambda i, b: (i, 0, 0))
    return pl.pallas_call(
        functools.partial(_dft_inner_kernel, kb=kb, n2=n2),
        grid=(n1 // kb, B),
        in_specs=[sig, sig, tab(n2, 1), tab(n2, 1), tab(2 * n2, c), _full(m_fwd), _full(m_inv)],
        out_specs=(sig, sig),
        out_shape=(jax.ShapeDtypeStruct(ar.shape, BF16), jax.ShapeDtypeStruct(ar.shape, BF16)),
        compiler_params=_params("parallel", "parallel"),
        name="dft_inner",
    )(ar, ai, twc, tws, kf, m_fwd, m_inv)


def _dft_outer_inv_kernel(br_ref, bi_ref, m_ref, u_ref, x0_ref, d_ref, o_ref):
    b = jnp.concatenate([br_ref[...].astype(BF16), bi_ref[...].astype(BF16)], axis=0)
    y = _dot(m_ref[...], b)
    u = u_ref[...].astype(F32)
    o_ref[...] = (x0_ref[...].astype(F32) * (y + u * d_ref[...])).astype(o_ref.dtype)


def _dft_outer_inv(br, bi, m_inv_outer, u2, x02, d_lanes):
    B, n1, W = br.shape
    lb = d_lanes.shape[1]
    blk = lambda rows: pl.BlockSpec((None, rows, lb), lambda b, j: (b, 0, j))
    return pl.pallas_call(
        _dft_outer_inv_kernel,
        grid=(B, W // lb),
        in_specs=[blk(n1), blk(n1), _full(m_inv_outer), blk(n1 // 2), blk(n1 // 2), _full(d_lanes)],
        out_specs=blk(n1 // 2),
        out_shape=jax.ShapeDtypeStruct((B, n1 // 2, W), BF16),
        compiler_params=_params("parallel", "parallel"),
        name="dft_outer_inv_gate",
    )(br, bi, m_inv_outer, u2, x02, d_lanes)


@functools.lru_cache(maxsize=None)
def _dft_tables(seq):
    n = 2 * seq
    n2 = DFT_N2
    n1 = n // n2
    k1 = np.arange(n1)[:, None]
    th = 2.0 * np.pi * ((k1 * np.arange(n1)[None, :]) % n1) / n1
    c1, s1 = np.cos(th), np.sin(th)
    m_outer_full = np.concatenate([c1, -s1], axis=0)
    m_outer_half = m_outer_full[:, :n1 // 2]
    m_outer_inv = np.concatenate([c1[:n1 // 2], -s1[:n1 // 2]], axis=1)
    ps = 2.0 * np.pi * ((np.arange(n2)[:, None] * np.arange(n2)[None, :]) % n2) / n2
    c2, s2 = np.cos(ps), np.sin(ps)
    m_fwd = np.block([[c2, s2], [-s2, c2]])
    m_inv = np.block([[c2, -s2], [s2, c2]])
    ph = 2.0 * np.pi * (k1 * np.arange(n2)[None, :]) / n
    f = lambda a: np.asarray(a, np.float32)
    return dict(n1=n1, n2=n2, m_outer_full=f(m_outer_full), m_outer_half=f(m_outer_half),
                m_outer_inv=f(m_outer_inv), m_fwd=f(m_fwd), m_inv=f(m_inv),
                twc=f(np.cos(ph))[:, :, None], tws=f(np.sin(ph))[:, :, None])


def _filter_features(seq):
    t = jnp.linspace(0.0, 1.0, seq, dtype=F32)[:, None]
    w = 2.0 * math.pi * jnp.arange(seq, dtype=F32) / seq
    f = jnp.linspace(1e-4, FILTER_BANDS - 1, FILTER_BANDS, dtype=F32)
    n = jnp.arange(2 * seq)
    lag = jnp.where(n < seq, n, jnp.minimum(2 * seq - n, seq - 1))
    t, w = t[lag], w[lag]
    ang = w[:, None] * f[None, :]
    z = jnp.concatenate([t, jnp.cos(ang), -jnp.sin(ang)], axis=-1)
    return jnp.pad(z, ((0, 0), (0, LANES - FILTER_EMB)))


def _hyena(phy, lp):
    B, L, W = phy.shape
    c = W // 3
    tb = _dft_tables(L)
    n1, n2 = tb["n1"], tb["n2"]
    bf = lambda name: jnp.asarray(tb[name]).astype(BF16)
    twc, tws = jnp.asarray(tb["twc"]), jnp.asarray(tb["tws"])

    u, x0 = _hygate(phy, lp["conv_w"], lp["conv_b"])

    kern = _filters(_filter_features(L), lp["filt_w1"], lp["filt_b1"], lp["filt_f1"], lp["filt_w2"],
                    lp["filt_b2"], lp["filt_f2"], lp["filt_w3"], lp["deltas"])
    far, fai = _dft_outer(kern.reshape(1, n1, n2 * c), bf("m_outer_full"), F32)
    kf = _dft_spectrum(far.reshape(n1, n2, c), fai.reshape(n1, n2, c), twc, tws, bf("m_fwd"),
                       scale=1.0 / (2 * L))

    u2 = u.reshape(B, n1 // 2, n2 * c)
    ar, ai = _dft_outer(u2, bf("m_outer_half"), BF16)
    br, bi = _dft_inner(ar.reshape(B, n1, n2, c), ai.reshape(B, n1, n2, c), twc, tws, kf,
                        bf("m_fwd"), bf("m_inv"))
    lb = min(DFT_LANE_BLOCK, n2 * c)
    d_lanes = jnp.tile(lp["hyena_d"], (1, lb // c))
    hyo = _dft_outer_inv(br.reshape(B, n1, n2 * c), bi.reshape(B, n1, n2 * c), bf("m_outer_inv"),
                         u2, x0.reshape(B, n1 // 2, n2 * c), d_lanes)
    return hyo.reshape(B, L, c)


def _outproj_kernel(att_ref, hyo_ref, x_ref, wa_ref, wh_ref, bo_ref, g_ref, b_ref, wr_hi_ref, wr_lo_ref,
                    br_ref, x1_ref, comb_ref, *, alpha):
    m = _dot(att_ref[...], wa_ref[...]) + _dot(hyo_ref[...], wh_ref[...]) + bo_ref[...]
    x1 = _layer_norm(alpha * x_ref[...] + m, g_ref[...], b_ref[...])
    x1_ref[...] = x1

    xh, xl = _split_bf16(x1)
    lg = _dot(xh, wr_hi_ref[...]) + (_dot(xh, wr_lo_ref[...]) + _dot(xl, wr_hi_ref[...])) + br_ref[...]
    lane = lax.broadcasted_iota(jnp.int32, lg.shape, 1)
    big = jnp.int32(ROUTER_LANES)
    ninf = jnp.float32(-jnp.inf)
    first = lambda mask: jnp.min(jnp.where(mask, lane, big), axis=-1, keepdims=True)

    gmask = lane < N_GROUPS
    gmax = jnp.max(jnp.where(gmask, lg, ninf), axis=-1, keepdims=True)
    gsum = jnp.sum(jnp.where(gmask, jnp.exp(lg - gmax), 0.0), axis=-1, keepdims=True)
    g_w = 1.0 / gsum
    g_idx = first(gmask & (lg == gmax))
    lo = N_GROUPS + EXPERTS_PER_GROUP * g_idx
    emask = (lane >= lo) & (lane < lo + EXPERTS_PER_GROUP)
    e1 = jnp.max(jnp.where(emask, lg, ninf), axis=-1, keepdims=True)
    i1 = first(emask & (lg == e1))
    rest = emask & (lane != i1)
    e2 = jnp.max(jnp.where(rest, lg, ninf), axis=-1, keepdims=True)
    i2 = first(rest & (lg == e2))
    tt = jnp.exp(e2 - e1)
    w1 = g_w / (1.0 + tt)
    comb_ref[...] = (jnp.where(lane == i1, w1, 0.0) + jnp.where(lane == i2, w1 * tt, 0.0)
                     + jnp.where(lane == 0, g_idx.astype(F32), 0.0))


def _outproj_ln_router(att, hyo, x, wa, wh, bo, g, b, wr_hi, wr_lo, br, *, alpha):
    B, L, D = x.shape
    tm = PROJ_TILE
    tok = lambda w: pl.BlockSpec((None, tm, w), lambda bb, i: (bb, i, 0))
    return pl.pallas_call(
        functools.partial(_outproj_kernel, alpha=alpha),
        grid=(B, L // tm),
        in_specs=[tok(att.shape[2]), tok(hyo.shape[2]), tok(D)]
        + [_full(a) for a in (wa, wh, bo, g, b, wr_hi, wr_lo, br)],
        out_specs=(tok(D), tok(ROUTER_LANES)),
        out_shape=(jax.ShapeDtypeStruct((B, L, D), F32), jax.ShapeDtypeStruct((B, L, ROUTER_LANES), F32)),
        compiler_params=_params("parallel", "parallel"),
        name="outproj_ln_router",
    )(att, hyo, x, wa, wh, bo, g, b, wr_hi, wr_lo, br)


def _moe_kernel(meta_ref, x_ref, comb_ref, w1_ref, w3_ref, w2_ref, g_ref, b_ref, o_ref,
                xs_ref, ys_ref, cs_ref, pt_ref, *, alpha, tb, slots, tile):
    bi = pl.program_id(0)
    blk = pl.program_id(1)
    e = pl.program_id(2)
    grp = e // EXPERTS_PER_GROUP

    @pl.when(e == 0)
    def _():
        comb = comb_ref[...]
        starts = [meta_ref[bi, blk, g].astype(F32) for g in range(N_GROUPS)]
        gid_row = comb.T[0:1, :]
        grow = lax.broadcasted_iota(jnp.int32, (8, tb), 0).astype(F32)
        oh_row = jnp.where(gid_row == grow, 1.0, 0.0)
        r_i = lax.broadcasted_iota(jnp.int32, (tb, tb), 0)
        c_i = lax.broadcasted_iota(jnp.int32, (tb, tb), 1)
        earlier_row = jnp.where(r_i < c_i, 1.0, 0.0).astype(BF16)
        rank_row = jnp.sum(oh_row * _dot(oh_row.astype(BF16), earlier_row), axis=0, keepdims=True)
        pos_row = rank_row
        for g in range(N_GROUPS):
            pos_row = pos_row + oh_row[g:g + 1] * starts[g]
        slot_r = lax.broadcasted_iota(jnp.int32, (slots, tb), 0).astype(F32)
        p = jnp.where(slot_r == pos_row, 1.0, 0.0).astype(BF16)
        xs_ref[...] = _dot(p, x_ref[...].astype(BF16)).astype(BF16)
        ch, cl = _split_bf16(comb)
        cs_ref[...] = _dot(p, ch) + _dot(p, cl)

        gid_col = comb[:, 0:1]
        glane = lax.broadcasted_iota(jnp.int32, comb.shape, 1).astype(F32)
        oh_col = jnp.where(gid_col == glane, 1.0, 0.0)
        earlier_col = jnp.where(c_i < r_i, 1.0, 0.0).astype(BF16)
        rank_col = jnp.sum(oh_col * _dot(earlier_col, oh_col.astype(BF16)), axis=1, keepdims=True)
        pos_col = rank_col
        for g in range(N_GROUPS):
            pos_col = pos_col + oh_col[:, g:g + 1] * starts[g]
        slot_c = lax.broadcasted_iota(jnp.int32, (tb, slots), 1).astype(F32)
        pt_ref[...] = jnp.where(slot_c == pos_col, 1.0, 0.0).astype(BF16)
        ys_ref[...] = jnp.zeros_like(ys_ref)

    start = meta_ref[bi, blk, grp]
    n_tiles = meta_ref[bi, blk, N_GROUPS + grp]

    def row_tile(i, carry):
        rows = pl.ds(pl.multiple_of(start + i * tile, MOE_GROUP_ALIGN), tile)
        xt = xs_ref[rows, :]
        hid = jax.nn.silu(_dot(xt, w1_ref[...])) * _dot(xt, w3_ref[...])
        y = _dot(hid.astype(BF16), w2_ref[...])
        c = cs_ref[rows, :]
        lane = lax.broadcasted_iota(jnp.int32, c.shape, 1)
        ce = jnp.sum(jnp.where(lane == N_GROUPS + e, c, 0.0), axis=-1, keepdims=True)
        ys_ref[rows, :] += ce * y
        return carry

    lax.fori_loop(0, n_tiles, row_tile, 0)

    @pl.when(e == pl.num_programs(2) - 1)
    def _():
        f = _dot(pt_ref[...], ys_ref[...].astype(BF16))
        o_ref[...] = _layer_norm(alpha * x_ref[...] + f, g_ref[...], b_ref[...])


def _moe_ln(x1, comb, w1, w3, w2, g, b, *, alpha):
    B, L, D = x1.shape
    E, _, de = w1.shape
    tb = min(MOE_BLOCK, L)
    tile = MOE_ROW_TILE
    nblk = L // tb
    slots = tb + 2 * tile
    gid = comb[..., 0].astype(jnp.int32).reshape(B, nblk, tb)
    cnt = jnp.sum(gid[..., None] == jnp.arange(N_GROUPS), axis=2).astype(jnp.int32)
    padded = (cnt + MOE_GROUP_ALIGN - 1) // MOE_GROUP_ALIGN * MOE_GROUP_ALIGN
    meta = jnp.concatenate([jnp.cumsum(padded, axis=-1) - padded, (cnt + tile - 1) // tile], axis=-1)

    tok = lambda w: pl.BlockSpec((None, tb, w), lambda bb, i, e, m: (bb, i, 0))
    const = lambda a: pl.BlockSpec(a.shape, lambda bb, i, e, m: (0,) * a.ndim)
    return pl.pallas_call(
        functools.partial(_moe_kernel, alpha=alpha, tb=tb, slots=slots, tile=tile),
        grid_spec=pltpu.PrefetchScalarGridSpec(
            num_scalar_prefetch=1,
            grid=(B, nblk, E),
            in_specs=[tok(D), tok(ROUTER_LANES),
                      pl.BlockSpec((None, D, de), lambda bb, i, e, m: (e, 0, 0)),
                      pl.BlockSpec((None, D, de), lambda bb, i, e, m: (e, 0, 0)),
                      pl.BlockSpec((None, de, D), lambda bb, i, e, m: (e, 0, 0)),
                      const(g), const(b)],
            out_specs=tok(D),
            scratch_shapes=[pltpu.VMEM((slots, D), BF16), pltpu.VMEM((slots, D), F32),
                            pltpu.VMEM((slots, ROUTER_LANES), F32), pltpu.VMEM((tb, slots), BF16)]),
        out_shape=jax.ShapeDtypeStruct((B, L, D), F32),
        compiler_params=_params("parallel", "parallel", "arbitrary"),
        name="moe_ln",
    )(meta, x1, comb, w1, w3, w2, g, b)


def _alibi_tables():
    t = ATT_TILE
    slopes = jnp.asarray(LOG2E * 2.0 ** (-8.0 * np.arange(1, ATT_HEADS + 1) / ATT_HEADS), dtype=F32)
    off = (jnp.arange(t)[:, None] - jnp.arange(t)[None, :]).astype(F32)
    dpos = slopes[:, None, None] * off[None]
    return slopes, jnp.stack([dpos, -jnp.abs(dpos), -dpos], axis=1)


def _prep_layer(l, depth, w_in, b_in, conv_w, conv_b, lam_q1, lam_k1, lam_q2, lam_k2, subln_g,
                filt_w1, filt_b1, filt_freq1, filt_w2, filt_b2, filt_freq2, filt_w3, hyena_d,
                w_out, b_out, ln1_g, ln1_b, router_group_w, router_group_b, router_expert_w,
                router_expert_b, exp_w1, exp_w3, exp_w2, ln2_g, ln2_b):
    a0, a1, a2 = ATT_QK_WIDTH, 2 * ATT_QK_WIDTH, 2 * ATT_QK_WIDTH + ATT_WIDTH
    scale = ATT_HEAD_DIM ** -0.5 * LOG2E
    row = lambda v: v.reshape(1, -1)
    w, b = w_in[l], b_in[l]
    c = hyena_d.shape[1]
    wr = jnp.concatenate([router_group_w[l], router_expert_w[l]], axis=1)
    wr = jnp.pad(wr, ((0, 0), (0, ROUTER_LANES - wr.shape[1])))
    wr_hi = wr.astype(BF16)
    br = jnp.concatenate([router_group_b[l], router_expert_b[l]])
    max_decay = abs(math.log(DECAY_TARGET) / FAST_DECAY_PCT)
    min_decay = abs(math.log(DECAY_TARGET) / SLOW_DECAY_PCT)
    return dict(
        lam_init=0.8 - 0.6 * math.exp(-0.3 * l),
        alpha=(2.0 * depth) ** 0.25,
        wq=(w[:, :a0] * scale).astype(BF16), bq=row(b[:a0] * scale),
        wk=w[:, a0:a1].astype(BF16), bk=row(b[a0:a1]),
        wvt=w[:, a1:a2].T.astype(BF16), bvt=b[a1:a2].reshape(-1, 1),
        why=w[:, a2:].astype(BF16), bhy=row(b[a2:]),
        conv_w=conv_w[l], conv_b=row(conv_b[l]),
        lamv=jnp.stack([lam_q1[l], lam_k1[l], lam_q2[l], lam_k2[l]]).astype(F32),
        subln_g=row(subln_g[l]),
        filt_w1=jnp.pad(filt_w1[l], ((0, LANES - FILTER_EMB), (0, 0))), filt_b1=row(filt_b1[l]),
        filt_f1=row(filt_freq1[l]), filt_w2=filt_w2[l], filt_b2=row(filt_b2[l]), filt_f2=row(filt_freq2[l]),
        filt_w3=filt_w3[l],
        deltas=row(jnp.linspace(min_decay, max_decay, c, dtype=F32)),
        hyena_d=row(hyena_d[l]),
        wa=w_out[l][:ATT_WIDTH].astype(BF16), wh=w_out[l][ATT_WIDTH:].astype(BF16), bo=row(b_out[l]),
        ln1_g=row(ln1_g[l]), ln1_b=row(ln1_b[l]),
        wr_hi=wr_hi, wr_lo=(wr - wr_hi.astype(F32)).astype(BF16),
        br=row(jnp.pad(br, (0, ROUTER_LANES - br.shape[0]))),
        w1=exp_w1[l].astype(BF16), w3=exp_w3[l].astype(BF16), w2=exp_w2[l].astype(BF16),
        ln2_g=row(ln2_g[l]), ln2_b=row(ln2_b[l]),
    )


def _layer(x, lp, alibi):
    slopes, dist = alibi
    q, k, vt, phy = _inproj(x, lp["wq"], lp["bq"], lp["wk"], lp["bk"], lp["wvt"], lp["bvt"],
                            lp["why"], lp["bhy"])
    att = _attention(q, k, vt, slopes, dist, lp["lamv"], lp["subln_g"],
                     lam_init=lp["lam_init"])
    hyo = _hyena(phy, lp)
    x1, comb = _outproj_ln_router(att, hyo, x, lp["wa"], lp["wh"], lp["bo"], lp["ln1_g"], lp["ln1_b"],
                                  lp["wr_hi"], lp["wr_lo"], lp["br"], alpha=lp["alpha"])
    return _moe_ln(x1, comb, lp["w1"], lp["w3"], lp["w2"], lp["ln2_g"], lp["ln2_b"], alpha=lp["alpha"])


def kernel(x_prompt, x_sample, w_in, b_in, conv_w, conv_b, lam_q1, lam_k1, lam_q2, lam_k2, subln_g, filt_w1, filt_b1, filt_freq1, filt_w2, filt_b2, filt_freq2, filt_w3, hyena_d, w_out, b_out, ln1_g, ln1_b, router_group_w, router_group_b, router_expert_w, router_expert_b, exp_w1, exp_w3, exp_w2, ln2_g, ln2_b):
    params = (w_in, b_in, conv_w, conv_b, lam_q1, lam_k1, lam_q2, lam_k2, subln_g, filt_w1, filt_b1,
              filt_freq1, filt_w2, filt_b2, filt_freq2, filt_w3, hyena_d, w_out, b_out, ln1_g, ln1_b,
              router_group_w, router_group_b, router_expert_w, router_expert_b, exp_w1, exp_w3, exp_w2,
              ln2_g, ln2_b)
    depth = w_in.shape[0]
    layers = [_prep_layer(l, depth, *params) for l in range(depth)]
    alibi = _alibi_tables()

    def trunk(x):
        for lp in layers:
            x = _layer(x, lp, alibi)
        return x

    return (trunk(x_prompt), trunk(x_sample))
```

```python
import functools
import math

import numpy as np
import jax
import jax.numpy as jnp
from jax import lax
from jax.experimental import pallas as pl
from jax.experimental.pallas import tpu as pltpu

F32 = jnp.float32
BF16 = jnp.bfloat16

ATT_HEADS = 4
ATT_HEAD_DIM = 64
ATT_V_DIM = 2 * ATT_HEAD_DIM
ATT_QK_WIDTH = ATT_HEADS * 2 * ATT_HEAD_DIM
ATT_WIDTH = ATT_HEADS * ATT_V_DIM
FILTER_EMB = 33
FILTER_BANDS = (FILTER_EMB - 1) // 2
DECAY_TARGET = 1e-2
FAST_DECAY_PCT = 0.3
SLOW_DECAY_PCT = 1.5
FILTER_SHIFT = 0.05
N_GROUPS = 4
EXPERTS_PER_GROUP = 4
N_EXPERTS = N_GROUPS * EXPERTS_PER_GROUP
LN_EPS = 1e-5
RMS_EPS = 1e-5

LANES = 128
BF16_SUBLANES = 16
VMEM_LIMIT_BYTES = 56 * 1024 * 1024

ATT_TILE = 256
ATT_Q_TILES = 1
ATT_KEY_BLOCK = 1024
VT_ROWS = ATT_V_DIM + BF16_SUBLANES
PROJ_TILE = 512
GATE_TILE = 512
FILTER_TILE = 512
DFT_N2 = 128
DFT_LANE_BLOCK = 2048
DFT_K1_BLOCK = 8
MOE_BLOCK = 1024
MOE_ROW_TILE = 128
MOE_GROUP_ALIGN = BF16_SUBLANES
MOE_EXPERTS_PER_STEP = 2
ROUTER_LANES = LANES
NEG_BIG = -1e30
LOG2E = math.log2(math.e)


def _params(*sem):
    return pltpu.CompilerParams(dimension_semantics=sem, vmem_limit_bytes=VMEM_LIMIT_BYTES)


def _full(a):
    nd = a.ndim
    return pl.BlockSpec(a.shape, lambda *_: (0,) * nd)


def _dot(a, b):
    return jnp.dot(a, b, preferred_element_type=F32)


def _dot_nt(a, b):
    return lax.dot_general(a, b, (((1,), (1,)), ((), ())), preferred_element_type=F32)


def _split_bf16(x):
    hi = x.astype(BF16)
    lo = (x - hi.astype(F32)).astype(BF16)
    return hi, lo


def _dot_3pass(a, b):
    ah, al = _split_bf16(a)
    bh, bl = _split_bf16(b)
    return _dot(ah, bh) + (_dot(ah, bl) + _dot(al, bh))


def _layer_norm(r, g, b):
    mu = jnp.mean(r, axis=-1, keepdims=True)
    c = r - mu
    var = jnp.mean(c * c, axis=-1, keepdims=True)
    return c * lax.rsqrt(var + LN_EPS) * g + b


def _inproj_kernel(x_ref, wq_ref, bq_ref, wk_ref, bk_ref, wvt_ref, bvt_ref, why_ref, bhy_ref,
                   q_ref, k_ref, vt_ref, phy_ref):
    xb = x_ref[...].astype(BF16)
    tm = xb.shape[0]
    q_ref[...] = (_dot(xb, wq_ref[...]) + bq_ref[...]).astype(BF16)
    k_ref[...] = (_dot(xb, wk_ref[...]) + bk_ref[...]).astype(BF16)
    vt = (_dot_nt(wvt_ref[...], xb) + bvt_ref[...]).astype(BF16)
    row = lax.broadcasted_iota(jnp.int32, (BF16_SUBLANES, tm), 0)
    ones_rows = jnp.where(row == 0, 1.0, 0.0).astype(BF16)
    for h in range(ATT_HEADS):
        vt_ref[h, :ATT_V_DIM, :] = vt[h * ATT_V_DIM:(h + 1) * ATT_V_DIM]
        vt_ref[h, ATT_V_DIM:, :] = ones_rows
    phy_ref[...] = (_dot(xb, why_ref[...]) + bhy_ref[...]).astype(phy_ref.dtype)


def _inproj(x, wq, bq, wk, bk, wvt, bvt, why, bhy):
    B, L, D = x.shape
    tm, kb = PROJ_TILE, min(ATT_KEY_BLOCK, L)
    per = kb // tm
    hyw = why.shape[1]
    tok = lambda w: pl.BlockSpec((None, tm, w), lambda b, i: (b, i, 0))
    return pl.pallas_call(
        _inproj_kernel,
        grid=(B, L // tm),
        in_specs=[tok(D)] + [_full(a) for a in (wq, bq, wk, bk, wvt, bvt, why, bhy)],
        out_specs=(tok(ATT_QK_WIDTH), tok(ATT_QK_WIDTH),
                   pl.BlockSpec((None, None, ATT_HEADS, VT_ROWS, tm),
                                lambda b, i: (b, i // per, 0, 0, i % per)),
                   tok(hyw)),
        out_shape=(jax.ShapeDtypeStruct((B, L, ATT_QK_WIDTH), BF16),
                   jax.ShapeDtypeStruct((B, L, ATT_QK_WIDTH), BF16),
                   jax.ShapeDtypeStruct((B, L // kb, ATT_HEADS, VT_ROWS, kb), BF16),
                   jax.ShapeDtypeStruct((B, L, hyw), BF16)),
        compiler_params=_params("parallel", "parallel"),
        name="inproj",
    )(x, wq, bq, wk, bk, wvt, bvt, why, bhy)


def _attn_kernel(slopes_ref, q_ref, k_ref, vt_ref, dist_ref, lam_ref, g_ref,
                 o_ref, acc_ref, s_ref, *, t, nq, kb, nkb, lam_init):
    h = pl.program_id(1)
    q0 = pl.program_id(2) * nq
    ns = kb // t
    slope = slopes_ref[h]
    q = q_ref[...]
    lane = lax.broadcasted_iota(jnp.int32, q.shape, 1)
    zero = jnp.zeros_like(q)
    qmaps = (jnp.where(lane < ATT_HEAD_DIM, q, zero), jnp.where(lane >= ATT_HEAD_DIM, q, zero))
    acc_ref[...] = jnp.zeros_like(acc_ref)

    def produce(qt, j, slot):
        kblk = k_ref[pl.ds(pl.multiple_of(j * kb, kb), kb), :]
        for mi in range(2):
            s = _dot_nt(kblk, qmaps[mi][qt * t:(qt + 1) * t])
            for st in range(ns):
                sel = jnp.clip(j * ns + st - (q0 + qt), -1, 1) + 1
                s_ref[qt, slot, mi, st * t:(st + 1) * t, :] = s[st * t:(st + 1) * t] + dist_ref[sel]

    def consume(qt, j, slot, ms):
        vt = vt_ref[j]
        cs = [-slope * (jnp.abs(j * ns + st - (q0 + qt)) * t).astype(F32) for st in range(ns)]
        out = []
        for mi in range(2):
            sub = [s_ref[qt, slot, mi, st * t:(st + 1) * t, :] for st in range(ns)]
            m_new = ms[mi]
            for st in range(ns):
                m_new = jnp.maximum(m_new, jnp.max(sub[st], axis=0, keepdims=True) + cs[st])
            e = jnp.concatenate([jnp.exp2(sub[st] - (m_new - cs[st])).astype(BF16) for st in range(ns)],
                                axis=0)
            acc_ref[qt, mi] = acc_ref[qt, mi] * jnp.exp2(ms[mi] - m_new) + _dot(vt, e)
            out.append(m_new)
        return tuple(out)

    def advance(j_next, slot_next, j, slot, ms):
        if j_next is not None:
            for qt in range(nq):
                produce(qt, j_next, slot_next)
        return tuple(consume(qt, j, slot, ms[qt]) for qt in range(nq))

    m0 = jnp.full((1, t), NEG_BIG, F32)
    ms = ((m0, m0),) * nq
    for qt in range(nq):
        produce(qt, 0, 0)
    if nkb > 1:
        def pair(i, ms):
            ms = advance(2 * i + 1, 1, 2 * i, 0, ms)
            return advance(2 * i + 2, 0, 2 * i + 1, 1, ms)

        ms = lax.fori_loop(0, nkb // 2 - 1, pair, ms)
        ms = advance(nkb - 1, 1, nkb - 2, 0, ms)
        advance(None, None, nkb - 1, 1, ms)
    else:
        advance(None, None, 0, 0, ms)

    lv = lam_ref[...]
    lam = (jnp.exp(jnp.sum(lv[0:1] * lv[1:2], axis=-1, keepdims=True))
           - jnp.exp(jnp.sum(lv[2:3] * lv[3:4], axis=-1, keepdims=True)) + lam_init)
    for qt in range(nq):
        a0 = acc_ref[qt, 0]
        a1 = acc_ref[qt, 1]
        o0 = a0[:ATT_V_DIM] / a0[ATT_V_DIM:ATT_V_DIM + 1]
        o1 = a1[:ATT_V_DIM] / a1[ATT_V_DIM:ATT_V_DIM + 1]
        d = o0 - lam * o1
        d = d * lax.rsqrt(jnp.mean(d * d, axis=0, keepdims=True) + RMS_EPS)
        o_ref[qt * t:(qt + 1) * t, :] = (d.T * (g_ref[...] * (1.0 - lam_init))).astype(o_ref.dtype)


def _attention(q, k, vt, slopes, dist, lamv, g, *, lam_init):
    B, L, _ = q.shape
    t = ATT_TILE
    nq = min(ATT_Q_TILES, L // t)
    nkb, kb = vt.shape[1], vt.shape[4]
    assert nkb == 1 or nkb % 2 == 0
    return pl.pallas_call(
        functools.partial(_attn_kernel, t=t, nq=nq, kb=kb, nkb=nkb, lam_init=lam_init),
        grid=(B, ATT_HEADS, L // (nq * t)),
        in_specs=[pl.BlockSpec(memory_space=pltpu.SMEM),
                  pl.BlockSpec((None, nq * t, LANES), lambda b, h, i: (b, i, h)),
                  pl.BlockSpec((None, L, LANES), lambda b, h, i: (b, 0, h)),
                  pl.BlockSpec((None, nkb, None, VT_ROWS, kb), lambda b, h, i: (b, 0, h, 0, 0)),
                  pl.BlockSpec((None, 3, t, t), lambda b, h, i: (h, 0, 0, 0)),
                  _full(lamv), _full(g)],
        out_specs=pl.BlockSpec((None, nq * t, LANES), lambda b, h, i: (b, i, h)),
        out_shape=jax.ShapeDtypeStruct((B, L, ATT_WIDTH), BF16),
        scratch_shapes=[pltpu.VMEM((nq, 2, VT_ROWS, t), F32), pltpu.VMEM((nq, 2, 2, kb, t), F32)],
        compiler_params=_params("parallel", "parallel", "arbitrary"),
        name="diff_attention",
    )(slopes, q, k, vt, dist, lamv, g)


def _hygate_kernel(p_ref, prev_ref, next_ref, w_ref, b_ref, u_ref, x0_ref, *, tl, c):
    i = pl.program_id(1)
    last = pl.num_programs(1) - 1
    x = p_ref[...].astype(F32)
    sub = prev_ref.shape[0]
    prev_row = jnp.where(i == 0, 0.0, prev_ref[...].astype(F32)[sub - 1:sub, :])
    next_row = jnp.where(i == last, 0.0, next_ref[...].astype(F32)[0:1, :])
    row = lax.broadcasted_iota(jnp.int32, x.shape, 0)
    xm = jnp.where(row == 0, prev_row, pltpu.roll(x, 1, 0))
    xp = jnp.where(row == tl - 1, next_row, pltpu.roll(x, tl - 1, 0))
    w = w_ref[...]
    hy = xm * w[0:1] + x * w[1:2] + xp * w[2:3] + b_ref[...]
    x0_ref[...] = hy[:, :c].astype(x0_ref.dtype)
    u_ref[...] = (hy[:, 2 * c:] * hy[:, c:2 * c]).astype(u_ref.dtype)


def _hygate(phy, conv_w, conv_b):
    B, L, W = phy.shape
    c = W // 3
    tl = GATE_TILE
    sub = BF16_SUBLANES
    nb = tl // sub
    last_blk = L // sub - 1
    return pl.pallas_call(
        functools.partial(_hygate_kernel, tl=tl, c=c),
        grid=(B, L // tl),
        in_specs=[pl.BlockSpec((None, tl, W), lambda b, i: (b, i, 0)),
                  pl.BlockSpec((None, sub, W), lambda b, i: (b, jnp.maximum(i * nb - 1, 0), 0)),
                  pl.BlockSpec((None, sub, W), lambda b, i: (b, jnp.minimum((i + 1) * nb, last_blk), 0)),
                  _full(conv_w), _full(conv_b)],
        out_specs=(pl.BlockSpec((None, tl, c), lambda b, i: (b, i, 0)),
                   pl.BlockSpec((None, tl, c), lambda b, i: (b, i, 0))),
        out_shape=(jax.ShapeDtypeStruct((B, L, c), BF16), jax.ShapeDtypeStruct((B, L, c), BF16)),
        compiler_params=_params("parallel", "parallel"),
        name="hyena_gate",
    )(phy, phy, phy, conv_w, conv_b)


def _filter_kernel(z_ref, w1_ref, b1_ref, f1_ref, w2_ref, b2_ref, f2_ref, w3_ref, deltas_ref,
                   k_ref, *, tl, seq, c):
    i = pl.program_id(0)
    h = jnp.sin(f1_ref[...] * (_dot_3pass(z_ref[...], w1_ref[...]) + b1_ref[...]))
    h = jnp.sin(f2_ref[...] * (_dot_3pass(h, w2_ref[...]) + b2_ref[...]))
    h = _dot_3pass(h, w3_ref[...])
    n = lax.broadcasted_iota(jnp.int32, (tl, c), 0) + i * tl
    lag = jnp.where(n < seq, n, 2 * seq - n).astype(F32)
    decay = jnp.exp(-(lag * (1.0 / (seq - 1))) * deltas_ref[...]) + FILTER_SHIFT
    k_ref[...] = jnp.where(n == seq, 0.0, h * decay)


def _filters(z2, w1, b1, f1, w2, b2, f2, w3, deltas):
    n = z2.shape[0]
    L = n // 2
    c = w3.shape[1] // 2
    tl = FILTER_TILE
    half = L // tl
    return pl.pallas_call(
        functools.partial(_filter_kernel, tl=tl, seq=L, c=c),
        grid=(n // tl,),
        in_specs=[pl.BlockSpec((tl, z2.shape[1]), lambda i: (i, 0))]
        + [_full(a) for a in (w1, b1, f1, w2, b2, f2)]
        + [pl.BlockSpec((w3.shape[0], c), lambda i: (0, i // half)), _full(deltas)],
        out_specs=pl.BlockSpec((tl, c), lambda i: (i, 0)),
        out_shape=jax.ShapeDtypeStruct((n, c), F32),
        compiler_params=_params("parallel"),
        name="hyena_filter",
    )(z2, w1, b1, f1, w2, b2, f2, w3, deltas)


def _dft_outer_kernel(u_ref, m_ref, ar_ref, ai_ref, *, n1):
    a = _dot(m_ref[...], u_ref[...].astype(BF16))
    ar_ref[...] = a[:n1].astype(ar_ref.dtype)
    ai_ref[...] = a[n1:].astype(ai_ref.dtype)


def _dft_outer(u2, m_outer, out_dtype):
    B, n1_in, W = u2.shape
    n1 = m_outer.shape[0] // 2
    lb = min(DFT_LANE_BLOCK, W)
    blk = lambda rows: pl.BlockSpec((None, rows, lb), lambda b, j: (b, 0, j))
    return pl.pallas_call(
        functools.partial(_dft_outer_kernel, n1=n1),
        grid=(B, W // lb),
        in_specs=[blk(n1_in), _full(m_outer)],
        out_specs=(blk(n1), blk(n1)),
        out_shape=(jax.ShapeDtypeStruct((B, n1, W), out_dtype), jax.ShapeDtypeStruct((B, n1, W), out_dtype)),
        compiler_params=_params("parallel", "parallel"),
        name="dft_outer",
    )(u2, m_outer)


def _twiddle(ar, ai, cph, sph, conj):
    if conj:
        return ar * cph - ai * sph, ai * cph + ar * sph
    return ar * cph + ai * sph, ai * cph - ar * sph


def _dft_spectrum_kernel(ar_ref, ai_ref, twc_ref, tws_ref, mf_ref, kf_ref, *, kb, n2, scale):
    for r in range(kb):
        pr, pi = _twiddle(ar_ref[r], ai_ref[r], twc_ref[r], tws_ref[r], conj=False)
        a = jnp.concatenate([pr.astype(BF16), pi.astype(BF16)], axis=0)
        kf_ref[r] = _dot(mf_ref[...], a) * scale


def _dft_spectrum(ar, ai, twc, tws, m_fwd, *, scale):
    n1, n2, c = ar.shape
    kb = DFT_K1_BLOCK
    blk = lambda rows, w: pl.BlockSpec((kb, rows, w), lambda i: (i, 0, 0))
    return pl.pallas_call(
        functools.partial(_dft_spectrum_kernel, kb=kb, n2=n2, scale=scale),
        grid=(n1 // kb,),
        in_specs=[blk(n2, c), blk(n2, c), blk(n2, 1), blk(n2, 1), _full(m_fwd)],
        out_specs=blk(2 * n2, c),
        out_shape=jax.ShapeDtypeStruct((n1, 2 * n2, c), F32),
        compiler_params=_params("parallel"),
        name="dft_filter_spectrum",
    )(ar, ai, twc, tws, m_fwd)


def _dft_inner_kernel(ar_ref, ai_ref, twc_ref, tws_ref, kf_ref, mf_ref, mi_ref, br_ref, bi_ref, *, kb, n2):
    for r in range(kb):
        cph = twc_ref[r]
        sph = tws_ref[r]
        pr, pi = _twiddle(ar_ref[r].astype(F32), ai_ref[r].astype(F32), cph, sph, conj=False)
        a = jnp.concatenate([pr.astype(BF16), pi.astype(BF16)], axis=0)
        x = _dot(mf_ref[...], a)
        xr, xi = x[:n2], x[n2:]
        kr, ki = kf_ref[r, :n2], kf_ref[r, n2:]
        y = jnp.concatenate([(xr * kr - xi * ki).astype(BF16), (xr * ki + xi * kr).astype(BF16)], axis=0)
        bb = _dot(mi_ref[...], y)
        qr, qi = _twiddle(bb[:n2], bb[n2:], cph, sph, conj=True)
        br_ref[r] = qr.astype(br_ref.dtype)
        bi_ref[r] = qi.astype(bi_ref.dtype)


def _dft_inner(ar, ai, twc, tws, kf, m_fwd, m_inv):
    B, n1, n2, c = ar.shape
    kb = DFT_K1_BLOCK
    sig = pl.BlockSpec((None, kb, n2, c), lambda i, b: (b, i, 0, 0))
    tab = lambda rows, w: pl.BlockSpec((kb, rows, w), lambda i, b: (i, 0, 0))
    return pl.pallas_call(
        functools.partial(_dft_inner_kernel, kb=kb, n2=n2),
        grid=(n1 // kb, B),
        in_specs=[sig, sig, tab(n2, 1), tab(n2, 1), tab(2 * n2, c), _full(m_fwd), _full(m_inv)],
        out_specs=(sig, sig),
        out_shape=(jax.ShapeDtypeStruct(ar.shape, BF16), jax.ShapeDtypeStruct(ar.shape, BF16)),
        compiler_params=_params("parallel", "parallel"),
        name="dft_inner",
    )(ar, ai, twc, tws, kf, m_fwd, m_inv)


def _dft_outer_inv_kernel(br_ref, bi_ref, m_ref, u_ref, x0_ref, d_ref, o_ref):
    b = jnp.concatenate([br_ref[...].astype(BF16), bi_ref[...].astype(BF16)], axis=0)
    y = _dot(m_ref[...], b)
    u = u_ref[...].astype(F32)
    o_ref[...] = (x0_ref[...].astype(F32) * (y + u * d_ref[...])).astype(o_ref.dtype)


def _dft_outer_inv(br, bi, m_inv_outer, u2, x02, d_lanes):
    B, n1, W = br.shape
    lb = d_lanes.shape[1]
    blk = lambda rows: pl.BlockSpec((None, rows, lb), lambda b, j: (b, 0, j))
    return pl.pallas_call(
        _dft_outer_inv_kernel,
        grid=(B, W // lb),
        in_specs=[blk(n1), blk(n1), _full(m_inv_outer), blk(n1 // 2), blk(n1 // 2), _full(d_lanes)],
        out_specs=blk(n1 // 2),
        out_shape=jax.ShapeDtypeStruct((B, n1 // 2, W), BF16),
        compiler_params=_params("parallel", "parallel"),
        name="dft_outer_inv_gate",
    )(br, bi, m_inv_outer, u2, x02, d_lanes)


@functools.lru_cache(maxsize=None)
def _dft_tables(seq):
    n = 2 * seq
    n2 = DFT_N2
    n1 = n // n2
    k1 = np.arange(n1)[:, None]
    th = 2.0 * np.pi * ((k1 * np.arange(n1)[None, :]) % n1) / n1
    c1, s1 = np.cos(th), np.sin(th)
    m_outer_full = np.concatenate([c1, -s1], axis=0)
    m_outer_half = m_outer_full[:, :n1 // 2]
    m_outer_inv = np.concatenate([c1[:n1 // 2], -s1[:n1 // 2]], axis=1)
    ps = 2.0 * np.pi * ((np.arange(n2)[:, None] * np.arange(n2)[None, :]) % n2) / n2
    c2, s2 = np.cos(ps), np.sin(ps)
    m_fwd = np.block([[c2, s2], [-s2, c2]])
    m_inv = np.block([[c2, -s2], [s2, c2]])
    ph = 2.0 * np.pi * (k1 * np.arange(n2)[None, :]) / n
    f = lambda a: np.asarray(a, np.float32)
    return dict(n1=n1, n2=n2, m_outer_full=f(m_outer_full), m_outer_half=f(m_outer_half),
                m_outer_inv=f(m_outer_inv), m_fwd=f(m_fwd), m_inv=f(m_inv),
                twc=f(np.cos(ph))[:, :, None], tws=f(np.sin(ph))[:, :, None])


def _filter_features(seq):
    t = jnp.linspace(0.0, 1.0, seq, dtype=F32)[:, None]
    w = 2.0 * math.pi * jnp.arange(seq, dtype=F32) / seq
    f = jnp.linspace(1e-4, FILTER_BANDS - 1, FILTER_BANDS, dtype=F32)
    ang = w[:, None] * f[None, :]
    z = jnp.concatenate([t, jnp.cos(ang), -jnp.sin(ang)], axis=-1)
    z = jnp.concatenate([z, z[seq - 1:], z[:0:-1]], axis=0)
    return jnp.pad(z, ((0, 0), (0, LANES - FILTER_EMB)))


def _hyena(phy, lp):
    B, L, W = phy.shape
    c = W // 3
    tb = _dft_tables(L)
    n1, n2 = tb["n1"], tb["n2"]
    bf = lambda name: jnp.asarray(tb[name]).astype(BF16)
    twc, tws = jnp.asarray(tb["twc"]), jnp.asarray(tb["tws"])

    u, x0 = _hygate(phy, lp["conv_w"], lp["conv_b"])

    kern = _filters(_filter_features(L), lp["filt_w1"], lp["filt_b1"], lp["filt_f1"], lp["filt_w2"],
                    lp["filt_b2"], lp["filt_f2"], lp["filt_w3"], lp["deltas"])
    far, fai = _dft_outer(kern.reshape(1, n1, n2 * c), bf("m_outer_full"), F32)
    kf = _dft_spectrum(far.reshape(n1, n2, c), fai.reshape(n1, n2, c), twc, tws, bf("m_fwd"),
                       scale=1.0 / (2 * L))

    u2 = u.reshape(B, n1 // 2, n2 * c)
    ar, ai = _dft_outer(u2, bf("m_outer_half"), BF16)
    br, bi = _dft_inner(ar.reshape(B, n1, n2, c), ai.reshape(B, n1, n2, c), twc, tws, kf,
                        bf("m_fwd"), bf("m_inv"))
    lb = min(DFT_LANE_BLOCK, n2 * c)
    d_lanes = jnp.tile(lp["hyena_d"], (1, lb // c))
    hyo = _dft_outer_inv(br.reshape(B, n1, n2 * c), bi.reshape(B, n1, n2 * c), bf("m_outer_inv"),
                         u2, x0.reshape(B, n1 // 2, n2 * c), d_lanes)
    return hyo.reshape(B, L, c)


def _outproj_kernel(att_ref, hyo_ref, x_ref, wa_ref, wh_ref, bo_ref, g_ref, b_ref, wr_hi_ref, wr_lo_ref,
                    br_ref, x1_ref, comb_ref, *, alpha):
    m = _dot(att_ref[...], wa_ref[...]) + _dot(hyo_ref[...], wh_ref[...]) + bo_ref[...]
    x1 = _layer_norm(alpha * x_ref[...] + m, g_ref[...], b_ref[...])
    x1_ref[...] = x1

    xh, xl = _split_bf16(x1)
    lg = _dot(xh, wr_hi_ref[...]) + (_dot(xh, wr_lo_ref[...]) + _dot(xl, wr_hi_ref[...])) + br_ref[...]
    lane = lax.broadcasted_iota(jnp.int32, lg.shape, 1)
    big = jnp.int32(ROUTER_LANES)
    ninf = jnp.float32(-jnp.inf)
    first = lambda mask: jnp.min(jnp.where(mask, lane, big), axis=-1, keepdims=True)

    gmask = lane < N_GROUPS
    gmax = jnp.max(jnp.where(gmask, lg, ninf), axis=-1, keepdims=True)
    gsum = jnp.sum(jnp.where(gmask, jnp.exp(lg - gmax), 0.0), axis=-1, keepdims=True)
    g_w = 1.0 / gsum
    g_idx = first(gmask & (lg == gmax))
    lo = N_GROUPS + EXPERTS_PER_GROUP * g_idx
    emask = (lane >= lo) & (lane < lo + EXPERTS_PER_GROUP)
    e1 = jnp.max(jnp.where(emask, lg, ninf), axis=-1, keepdims=True)
    i1 = first(emask & (lg == e1))
    rest = emask & (lane != i1)
    e2 = jnp.max(jnp.where(rest, lg, ninf), axis=-1, keepdims=True)
    i2 = first(rest & (lg == e2))
    tt = jnp.exp(e2 - e1)
    w1 = g_w / (1.0 + tt)
    comb_ref[...] = (jnp.where(lane == i1, w1, 0.0) + jnp.where(lane == i2, w1 * tt, 0.0)
                     + jnp.where(lane == 0, g_idx.astype(F32), 0.0))


def _outproj_ln_router(att, hyo, x, wa, wh, bo, g, b, wr_hi, wr_lo, br, *, alpha):
    B, L, D = x.shape
    tm = PROJ_TILE
    tok = lambda w: pl.BlockSpec((None, tm, w), lambda bb, i: (bb, i, 0))
    return pl.pallas_call(
        functools.partial(_outproj_kernel, alpha=alpha),
        grid=(B, L // tm),
        in_specs=[tok(att.shape[2]), tok(hyo.shape[2]), tok(D)]
        + [_full(a) for a in (wa, wh, bo, g, b, wr_hi, wr_lo, br)],
        out_specs=(tok(D), tok(ROUTER_LANES)),
        out_shape=(jax.ShapeDtypeStruct((B, L, D), F32), jax.ShapeDtypeStruct((B, L, ROUTER_LANES), F32)),
        compiler_params=_params("parallel", "parallel"),
        name="outproj_ln_router",
    )(att, hyo, x, wa, wh, bo, g, b, wr_hi, wr_lo, br)


def _moe_kernel(meta_ref, x_ref, comb_ref, w1_ref, w3_ref, w2_ref, g_ref, b_ref, o_ref,
                xs_ref, ys_ref, cs_ref, pt_ref, *, alpha, tb, slots, tile):
    bi = pl.program_id(0)
    blk = pl.program_id(1)
    step = pl.program_id(2)
    grp = step // (EXPERTS_PER_GROUP // MOE_EXPERTS_PER_STEP)

    @pl.when(step == 0)
    def _():
        comb = comb_ref[...]
        starts = [meta_ref[bi, blk, g].astype(F32) for g in range(N_GROUPS)]
        gid_row = comb.T[0:1, :]
        grow = lax.broadcasted_iota(jnp.int32, (8, tb), 0).astype(F32)
        oh_row = jnp.where(gid_row == grow, 1.0, 0.0)
        r_i = lax.broadcasted_iota(jnp.int32, (tb, tb), 0)
        c_i = lax.broadcasted_iota(jnp.int32, (tb, tb), 1)
        earlier_row = jnp.where(r_i < c_i, 1.0, 0.0).astype(BF16)
        rank_row = jnp.sum(oh_row * _dot(oh_row.astype(BF16), earlier_row), axis=0, keepdims=True)
        pos_row = rank_row
        for g in range(N_GROUPS):
            pos_row = pos_row + oh_row[g:g + 1] * starts[g]
        slot_r = lax.broadcasted_iota(jnp.int32, (slots, tb), 0).astype(F32)
        p = jnp.where(slot_r == pos_row, 1.0, 0.0).astype(BF16)
        xs_ref[...] = _dot(p, x_ref[...].astype(BF16)).astype(BF16)
        ch, cl = _split_bf16(comb)
        cs_ref[...] = _dot(p, ch) + _dot(p, cl)

        gid_col = comb[:, 0:1]
        glane = lax.broadcasted_iota(jnp.int32, comb.shape, 1).astype(F32)
        oh_col = jnp.where(gid_col == glane, 1.0, 0.0)
        earlier_col = jnp.where(c_i < r_i, 1.0, 0.0).astype(BF16)
        rank_col = jnp.sum(oh_col * _dot(earlier_col, oh_col.astype(BF16)), axis=1, keepdims=True)
        pos_col = rank_col
        for g in range(N_GROUPS):
            pos_col = pos_col + oh_col[:, g:g + 1] * starts[g]
        slot_c = lax.broadcasted_iota(jnp.int32, (tb, slots), 1).astype(F32)
        pt_ref[...] = jnp.where(slot_c == pos_col, 1.0, 0.0).astype(BF16)
        ys_ref[...] = jnp.zeros_like(ys_ref)

    start = meta_ref[bi, blk, grp]
    n_tiles = meta_ref[bi, blk, N_GROUPS + grp]

    def row_tile(i, carry):
        rows = pl.ds(pl.multiple_of(start + i * tile, MOE_GROUP_ALIGN), tile)
        xt = xs_ref[rows, :]
        c = cs_ref[rows, :]
        lane = lax.broadcasted_iota(jnp.int32, c.shape, 1)
        hid = [jax.nn.silu(_dot(xt, w1_ref[k])) * _dot(xt, w3_ref[k]) for k in range(MOE_EXPERTS_PER_STEP)]
        y = None
        for k in range(MOE_EXPERTS_PER_STEP):
            e = N_GROUPS + step * MOE_EXPERTS_PER_STEP + k
            ce = jnp.sum(jnp.where(lane == e, c, 0.0), axis=-1, keepdims=True)
            yk = ce * _dot(hid[k].astype(BF16), w2_ref[k])
            y = yk if y is None else y + yk
        ys_ref[rows, :] += y
        return carry

    lax.fori_loop(0, n_tiles, row_tile, 0)

    @pl.when(step == pl.num_programs(2) - 1)
    def _():
        f = _dot(pt_ref[...], ys_ref[...].astype(BF16))
        o_ref[...] = _layer_norm(alpha * x_ref[...] + f, g_ref[...], b_ref[...])


def _moe_ln(x1, comb, w1, w3, w2, g, b, *, alpha):
    B, L, D = x1.shape
    E, _, de = w1.shape
    tb = min(MOE_BLOCK, L)
    tile = MOE_ROW_TILE
    per = MOE_EXPERTS_PER_STEP
    nblk = L // tb
    slots = tb + 2 * tile
    gid = comb[..., 0].astype(jnp.int32).reshape(B, nblk, tb)
    cnt = jnp.sum(gid[..., None] == jnp.arange(N_GROUPS), axis=2).astype(jnp.int32)
    padded = (cnt + MOE_GROUP_ALIGN - 1) // MOE_GROUP_ALIGN * MOE_GROUP_ALIGN
    meta = jnp.concatenate([jnp.cumsum(padded, axis=-1) - padded, (cnt + tile - 1) // tile], axis=-1)

    tok = lambda w: pl.BlockSpec((None, tb, w), lambda bb, i, e, m: (bb, i, 0))
    const = lambda a: pl.BlockSpec(a.shape, lambda bb, i, e, m: (0,) * a.ndim)
    return pl.pallas_call(
        functools.partial(_moe_kernel, alpha=alpha, tb=tb, slots=slots, tile=tile),
        grid_spec=pltpu.PrefetchScalarGridSpec(
            num_scalar_prefetch=1,
            grid=(B, nblk, E // per),
            in_specs=[tok(D), tok(ROUTER_LANES),
                      pl.BlockSpec((per, D, de), lambda bb, i, s, m: (s, 0, 0)),
                      pl.BlockSpec((per, D, de), lambda bb, i, s, m: (s, 0, 0)),
                      pl.BlockSpec((per, de, D), lambda bb, i, s, m: (s, 0, 0)),
                      const(g), const(b)],
            out_specs=tok(D),
            scratch_shapes=[pltpu.VMEM((slots, D), BF16), pltpu.VMEM((slots, D), F32),
                            pltpu.VMEM((slots, ROUTER_LANES), F32), pltpu.VMEM((tb, slots), BF16)]),
        out_shape=jax.ShapeDtypeStruct((B, L, D), F32),
        compiler_params=_params("parallel", "parallel", "arbitrary"),
        name="moe_ln",
    )(meta, x1, comb, w1, w3, w2, g, b)


def _alibi_tables():
    t = ATT_TILE
    slopes = jnp.asarray(LOG2E * 2.0 ** (-8.0 * np.arange(1, ATT_HEADS + 1) / ATT_HEADS), dtype=F32)
    off = (jnp.arange(t)[:, None] - jnp.arange(t)[None, :]).astype(F32)
    dpos = slopes[:, None, None] * off[None]
    return slopes, jnp.stack([dpos, -jnp.abs(dpos), -dpos], axis=1)


def _prep_layer(l, depth, w_in, b_in, conv_w, conv_b, lam_q1, lam_k1, lam_q2, lam_k2, subln_g,
                filt_w1, filt_b1, filt_freq1, filt_w2, filt_b2, filt_freq2, filt_w3, hyena_d,
                w_out, b_out, ln1_g, ln1_b, router_group_w, router_group_b, router_expert_w,
                router_expert_b, exp_w1, exp_w3, exp_w2, ln2_g, ln2_b):
    a0, a1, a2 = ATT_QK_WIDTH, 2 * ATT_QK_WIDTH, 2 * ATT_QK_WIDTH + ATT_WIDTH
    scale = ATT_HEAD_DIM ** -0.5 * LOG2E
    row = lambda v: v.reshape(1, -1)
    w, b = w_in[l], b_in[l]
    c = hyena_d.shape[1]
    wr = jnp.concatenate([router_group_w[l], router_expert_w[l]], axis=1)
    wr = jnp.pad(wr, ((0, 0), (0, ROUTER_LANES - wr.shape[1])))
    wr_hi = wr.astype(BF16)
    br = jnp.concatenate([router_group_b[l], router_expert_b[l]])
    max_decay = abs(math.log(DECAY_TARGET) / FAST_DECAY_PCT)
    min_decay = abs(math.log(DECAY_TARGET) / SLOW_DECAY_PCT)
    return dict(
        lam_init=0.8 - 0.6 * math.exp(-0.3 * l),
        alpha=(2.0 * depth) ** 0.25,
        wq=(w[:, :a0] * scale).astype(BF16), bq=row(b[:a0] * scale),
        wk=w[:, a0:a1].astype(BF16), bk=row(b[a0:a1]),
        wvt=w[:, a1:a2].T.astype(BF16), bvt=b[a1:a2].reshape(-1, 1),
        why=w[:, a2:].astype(BF16), bhy=row(b[a2:]),
        conv_w=conv_w[l], conv_b=row(conv_b[l]),
        lamv=jnp.stack([lam_q1[l], lam_k1[l], lam_q2[l], lam_k2[l]]).astype(F32),
        subln_g=row(subln_g[l]),
        filt_w1=jnp.pad(filt_w1[l], ((0, LANES - FILTER_EMB), (0, 0))), filt_b1=row(filt_b1[l]),
        filt_f1=row(filt_freq1[l]), filt_w2=filt_w2[l], filt_b2=row(filt_b2[l]), filt_f2=row(filt_freq2[l]),
        filt_w3=filt_w3[l],
        deltas=row(jnp.linspace(min_decay, max_decay, c, dtype=F32)),
        hyena_d=row(hyena_d[l]),
        wa=w_out[l][:ATT_WIDTH].astype(BF16), wh=w_out[l][ATT_WIDTH:].astype(BF16), bo=row(b_out[l]),
        ln1_g=row(ln1_g[l]), ln1_b=row(ln1_b[l]),
        wr_hi=wr_hi, wr_lo=(wr - wr_hi.astype(F32)).astype(BF16),
        br=row(jnp.pad(br, (0, ROUTER_LANES - br.shape[0]))),
        w1=exp_w1[l].astype(BF16), w3=exp_w3[l].astype(BF16), w2=exp_w2[l].astype(BF16),
        ln2_g=row(ln2_g[l]), ln2_b=row(ln2_b[l]),
    )


def _layer(x, lp, alibi):
    slopes, dist = alibi
    q, k, vt, phy = _inproj(x, lp["wq"], lp["bq"], lp["wk"], lp["bk"], lp["wvt"], lp["bvt"],
                            lp["why"], lp["bhy"])
    att = _attention(q, k, vt, slopes, dist, lp["lamv"], lp["subln_g"],
                     lam_init=lp["lam_init"])
    hyo = _hyena(phy, lp)
    x1, comb = _outproj_ln_router(att, hyo, x, lp["wa"], lp["wh"], lp["bo"], lp["ln1_g"], lp["ln1_b"],
                                  lp["wr_hi"], lp["wr_lo"], lp["br"], alpha=lp["alpha"])
    return _moe_ln(x1, comb, lp["w1"], lp["w3"], lp["w2"], lp["ln2_g"], lp["ln2_b"], alpha=lp["alpha"])


def kernel(x_prompt, x_sample, w_in, b_in, conv_w, conv_b, lam_q1, lam_k1, lam_q2, lam_k2, subln_g, filt_w1, filt_b1, filt_freq1, filt_w2, filt_b2, filt_freq2, filt_w3, hyena_d, w_out, b_out, ln1_g, ln1_b, router_group_w, router_group_b, router_expert_w, router_expert_b, exp_w1, exp_w3, exp_w2, ln2_g, ln2_b):
    params = (w_in, b_in, conv_w, conv_b, lam_q1, lam_k1, lam_q2, lam_k2, subln_g, filt_w1, filt_b1,
              filt_freq1, filt_w2, filt_b2, filt_freq2, filt_w3, hyena_d, w_out, b_out, ln1_g, ln1_b,
              router_group_w, router_group_b, router_expert_w, router_expert_b, exp_w1, exp_w3, exp_w2,
              ln2_g, ln2_b)
    depth = w_in.shape[0]
    layers = [_prep_layer(l, depth, *params) for l in range(depth)]
    alibi = _alibi_tables()

    def trunk(x):
        for lp in layers:
            x = _layer(x, lp, alibi)
        return x

    return (trunk(x_prompt), trunk(x_sample))
```

```python
import functools
import math

import numpy as np
import jax
import jax.numpy as jnp
from jax import lax
from jax.experimental import pallas as pl
from jax.experimental.pallas import tpu as pltpu

F32 = jnp.float32
BF16 = jnp.bfloat16

ATT_HEADS = 4
ATT_HEAD_DIM = 64
ATT_V_DIM = 2 * ATT_HEAD_DIM
ATT_QK_WIDTH = ATT_HEADS * 2 * ATT_HEAD_DIM
ATT_WIDTH = ATT_HEADS * ATT_V_DIM
FILTER_EMB = 33
FILTER_BANDS = (FILTER_EMB - 1) // 2
DECAY_TARGET = 1e-2
FAST_DECAY_PCT = 0.3
SLOW_DECAY_PCT = 1.5
FILTER_SHIFT = 0.05
N_GROUPS = 4
EXPERTS_PER_GROUP = 4
N_EXPERTS = N_GROUPS * EXPERTS_PER_GROUP
LN_EPS = 1e-5
RMS_EPS = 1e-5

LANES = 128
BF16_SUBLANES = 16
VMEM_LIMIT_BYTES = 56 * 1024 * 1024

ATT_TILE = 256
ATT_Q_TILES = 1
ATT_KEY_BLOCK = 1024
ATT_LOOP_BLOCKS = 4
VT_ROWS = ATT_V_DIM + BF16_SUBLANES
PROJ_TILE = 512
GATE_TILE = 512
FILTER_TILE = 512
DFT_N2 = 128
DFT_LANE_BLOCK = 2048
DFT_K1_BLOCK = 8
MOE_BLOCK = 1024
MOE_ROW_TILE = 128
MOE_GROUP_ALIGN = BF16_SUBLANES
MOE_EXPERTS_PER_STEP = 2
ROUTER_LANES = LANES
NEG_BIG = -1e30
LOG2E = math.log2(math.e)


def _params(*sem):
    return pltpu.CompilerParams(dimension_semantics=sem, vmem_limit_bytes=VMEM_LIMIT_BYTES)


def _full(a):
    nd = a.ndim
    return pl.BlockSpec(a.shape, lambda *_: (0,) * nd)


def _dot(a, b):
    return jnp.dot(a, b, preferred_element_type=F32)


def _dot_nt(a, b):
    return lax.dot_general(a, b, (((1,), (1,)), ((), ())), preferred_element_type=F32)


def _split_bf16(x):
    hi = x.astype(BF16)
    lo = (x - hi.astype(F32)).astype(BF16)
    return hi, lo


def _dot_3pass(a, b):
    ah, al = _split_bf16(a)
    bh, bl = _split_bf16(b)
    return _dot(ah, bh) + (_dot(ah, bl) + _dot(al, bh))


def _layer_norm(r, g, b):
    mu = jnp.mean(r, axis=-1, keepdims=True)
    c = r - mu
    var = jnp.mean(c * c, axis=-1, keepdims=True)
    return c * lax.rsqrt(var + LN_EPS) * g + b


def _inproj_kernel(x_ref, wq_ref, bq_ref, wk_ref, bk_ref, wvt_ref, bvt_ref, why_ref, bhy_ref,
                   q_ref, k_ref, vt_ref, phy_ref):
    xb = x_ref[...].astype(BF16)
    tm = xb.shape[0]
    q_ref[...] = (_dot(xb, wq_ref[...]) + bq_ref[...]).astype(BF16)
    k_ref[...] = (_dot(xb, wk_ref[...]) + bk_ref[...]).astype(BF16)
    vt = (_dot_nt(wvt_ref[...], xb) + bvt_ref[...]).astype(BF16)
    row = lax.broadcasted_iota(jnp.int32, (BF16_SUBLANES, tm), 0)
    ones_rows = jnp.where(row == 0, 1.0, 0.0).astype(BF16)
    for h in range(ATT_HEADS):
        vt_ref[h, :ATT_V_DIM, :] = vt[h * ATT_V_DIM:(h + 1) * ATT_V_DIM]
        vt_ref[h, ATT_V_DIM:, :] = ones_rows
    phy_ref[...] = (_dot(xb, why_ref[...]) + bhy_ref[...]).astype(phy_ref.dtype)


def _inproj(x, wq, bq, wk, bk, wvt, bvt, why, bhy):
    B, L, D = x.shape
    tm, kb = PROJ_TILE, min(ATT_KEY_BLOCK, L)
    per = kb // tm
    hyw = why.shape[1]
    tok = lambda w: pl.BlockSpec((None, tm, w), lambda b, i: (b, i, 0))
    return pl.pallas_call(
        _inproj_kernel,
        grid=(B, L // tm),
        in_specs=[tok(D)] + [_full(a) for a in (wq, bq, wk, bk, wvt, bvt, why, bhy)],
        out_specs=(tok(ATT_QK_WIDTH), tok(ATT_QK_WIDTH),
                   pl.BlockSpec((None, None, ATT_HEADS, VT_ROWS, tm),
                                lambda b, i: (b, i // per, 0, 0, i % per)),
                   tok(hyw)),
        out_shape=(jax.ShapeDtypeStruct((B, L, ATT_QK_WIDTH), BF16),
                   jax.ShapeDtypeStruct((B, L, ATT_QK_WIDTH), BF16),
                   jax.ShapeDtypeStruct((B, L // kb, ATT_HEADS, VT_ROWS, kb), BF16),
                   jax.ShapeDtypeStruct((B, L, hyw), BF16)),
        compiler_params=_params("parallel", "parallel"),
        name="inproj",
    )(x, wq, bq, wk, bk, wvt, bvt, why, bhy)


def _attn_kernel(slopes_ref, q_ref, k_ref, vt_ref, dist_ref, lam_ref, g_ref,
                 o_ref, acc_ref, s_ref, *, t, nq, kb, nkb, lam_init):
    h = pl.program_id(1)
    q0 = pl.program_id(2) * nq
    ns = kb // t
    slope = slopes_ref[h]
    q = q_ref[...]
    lane = lax.broadcasted_iota(jnp.int32, q.shape, 1)
    zero = jnp.zeros_like(q)
    qmaps = (jnp.where(lane < ATT_HEAD_DIM, q, zero), jnp.where(lane >= ATT_HEAD_DIM, q, zero))
    acc_ref[...] = jnp.zeros_like(acc_ref)

    def produce(qt, j, slot):
        kblk = k_ref[pl.ds(pl.multiple_of(j * kb, kb), kb), :]
        for mi in range(2):
            s = _dot_nt(kblk, qmaps[mi][qt * t:(qt + 1) * t])
            for st in range(ns):
                sel = jnp.clip(j * ns + st - (q0 + qt), -1, 1) + 1
                s_ref[qt, slot, mi, st * t:(st + 1) * t, :] = s[st * t:(st + 1) * t] + dist_ref[sel]

    def consume(qt, j, slot, ms):
        vt = vt_ref[j]
        cs = [-slope * (jnp.abs(j * ns + st - (q0 + qt)) * t).astype(F32) for st in range(ns)]
        out = []
        for mi in range(2):
            sub = [s_ref[qt, slot, mi, st * t:(st + 1) * t, :] for st in range(ns)]
            m_new = ms[mi]
            for st in range(ns):
                m_new = jnp.maximum(m_new, jnp.max(sub[st], axis=0, keepdims=True) + cs[st])
            e = jnp.concatenate([jnp.exp2(sub[st] - (m_new - cs[st])).astype(BF16) for st in range(ns)],
                                axis=0)
            acc_ref[qt, mi] = acc_ref[qt, mi] * jnp.exp2(ms[mi] - m_new) + _dot(vt, e)
            out.append(m_new)
        return tuple(out)

    def advance(j_next, slot_next, j, slot, ms):
        if j_next is not None:
            for qt in range(nq):
                produce(qt, j_next, slot_next)
        return tuple(consume(qt, j, slot, ms[qt]) for qt in range(nq))

    m0 = jnp.full((1, t), NEG_BIG, F32)
    ms = ((m0, m0),) * nq
    for qt in range(nq):
        produce(qt, 0, 0)
    def run(first, count, last, ms):
        for b in range(count):
            j = first + b
            nxt = None if (last and b == count - 1) else j + 1
            ms = advance(nxt, None if nxt is None else (b + 1) % 2, j, b % 2, ms)
        return ms

    per = ATT_LOOP_BLOCKS if nkb >= 2 * ATT_LOOP_BLOCKS else min(2, nkb)
    assert per % 2 == 0 or nkb == 1
    assert nkb % per == 0
    trips = nkb // per - 1
    if trips > 0:
        trips = jnp.minimum(pl.program_id(0) + trips, trips)
        ms = lax.fori_loop(0, trips, lambda i, ms: run(i * per, per, False, ms), ms)
    run(nkb - per, per, True, ms)

    lv = lam_ref[...]
    lam = (jnp.exp(jnp.sum(lv[0:1] * lv[1:2], axis=-1, keepdims=True))
           - jnp.exp(jnp.sum(lv[2:3] * lv[3:4], axis=-1, keepdims=True)) + lam_init)
    for qt in range(nq):
        a0 = acc_ref[qt, 0]
        a1 = acc_ref[qt, 1]
        o0 = a0[:ATT_V_DIM] / a0[ATT_V_DIM:ATT_V_DIM + 1]
        o1 = a1[:ATT_V_DIM] / a1[ATT_V_DIM:ATT_V_DIM + 1]
        d = o0 - lam * o1
        d = d * lax.rsqrt(jnp.mean(d * d, axis=0, keepdims=True) + RMS_EPS)
        o_ref[qt * t:(qt + 1) * t, :] = (d.T * (g_ref[...] * (1.0 - lam_init))).astype(o_ref.dtype)


def _attention(q, k, vt, slopes, dist, lamv, g, *, lam_init):
    B, L, _ = q.shape
    t = ATT_TILE
    nq = min(ATT_Q_TILES, L // t)
    nkb, kb = vt.shape[1], vt.shape[4]
    assert nkb == 1 or nkb % 2 == 0
    return pl.pallas_call(
        functools.partial(_attn_kernel, t=t, nq=nq, kb=kb, nkb=nkb, lam_init=lam_init),
        grid=(B, ATT_HEADS, L // (nq * t)),
        in_specs=[pl.BlockSpec(memory_space=pltpu.SMEM),
                  pl.BlockSpec((None, nq * t, LANES), lambda b, h, i: (b, i, h)),
                  pl.BlockSpec((None, L, LANES), lambda b, h, i: (b, 0, h)),
                  pl.BlockSpec((None, nkb, None, VT_ROWS, kb), lambda b, h, i: (b, 0, h, 0, 0)),
                  pl.BlockSpec((None, 3, t, t), lambda b, h, i: (h, 0, 0, 0)),
                  _full(lamv), _full(g)],
        out_specs=pl.BlockSpec((None, nq * t, LANES), lambda b, h, i: (b, i, h)),
        out_shape=jax.ShapeDtypeStruct((B, L, ATT_WIDTH), BF16),
        scratch_shapes=[pltpu.VMEM((nq, 2, VT_ROWS, t), F32), pltpu.VMEM((nq, 2, 2, kb, t), F32)],
        compiler_params=_params("parallel", "parallel", "arbitrary"),
        name="diff_attention",
    )(slopes, q, k, vt, dist, lamv, g)


def _hygate_kernel(p_ref, prev_ref, next_ref, w_ref, b_ref, u_ref, x0_ref, *, tl, c):
    i = pl.program_id(1)
    last = pl.num_programs(1) - 1
    x = p_ref[...].astype(F32)
    sub = prev_ref.shape[0]
    prev_row = jnp.where(i == 0, 0.0, prev_ref[...].astype(F32)[sub - 1:sub, :])
    next_row = jnp.where(i == last, 0.0, next_ref[...].astype(F32)[0:1, :])
    row = lax.broadcasted_iota(jnp.int32, x.shape, 0)
    xm = jnp.where(row == 0, prev_row, pltpu.roll(x, 1, 0))
    xp = jnp.where(row == tl - 1, next_row, pltpu.roll(x, tl - 1, 0))
    w = w_ref[...]
    hy = xm * w[0:1] + x * w[1:2] + xp * w[2:3] + b_ref[...]
    x0_ref[...] = hy[:, :c].astype(x0_ref.dtype)
    u_ref[...] = (hy[:, 2 * c:] * hy[:, c:2 * c]).astype(u_ref.dtype)


def _hygate(phy, conv_w, conv_b):
    B, L, W = phy.shape
    c = W // 3
    tl = GATE_TILE
    sub = BF16_SUBLANES
    nb = tl // sub
    last_blk = L // sub - 1
    return pl.pallas_call(
        functools.partial(_hygate_kernel, tl=tl, c=c),
        grid=(B, L // tl),
        in_specs=[pl.BlockSpec((None, tl, W), lambda b, i: (b, i, 0)),
                  pl.BlockSpec((None, sub, W), lambda b, i: (b, jnp.maximum(i * nb - 1, 0), 0)),
                  pl.BlockSpec((None, sub, W), lambda b, i: (b, jnp.minimum((i + 1) * nb, last_blk), 0)),
                  _full(conv_w), _full(conv_b)],
        out_specs=(pl.BlockSpec((None, tl, c), lambda b, i: (b, i, 0)),
                   pl.BlockSpec((None, tl, c), lambda b, i: (b, i, 0))),
        out_shape=(jax.ShapeDtypeStruct((B, L, c), BF16), jax.ShapeDtypeStruct((B, L, c), BF16)),
        compiler_params=_params("parallel", "parallel"),
        name="hyena_gate",
    )(phy, phy, phy, conv_w, conv_b)


def _filter_kernel(z_ref, w1_ref, b1_ref, f1_ref, w2_ref, b2_ref, f2_ref, w3_ref, deltas_ref,
                   k_ref, *, tl, seq, c):
    i = pl.program_id(0)
    h = jnp.sin(f1_ref[...] * (_dot_3pass(z_ref[...], w1_ref[...]) + b1_ref[...]))
    h = jnp.sin(f2_ref[...] * (_dot_3pass(h, w2_ref[...]) + b2_ref[...]))
    h = _dot_3pass(h, w3_ref[...])
    n = lax.broadcasted_iota(jnp.int32, (tl, c), 0) + i * tl
    lag = jnp.where(n < seq, n, 2 * seq - n).astype(F32)
    decay = jnp.exp(-(lag * (1.0 / (seq - 1))) * deltas_ref[...]) + FILTER_SHIFT
    k_ref[...] = jnp.where(n == seq, 0.0, h * decay)


def _filters(z2, w1, b1, f1, w2, b2, f2, w3, deltas):
    n = z2.shape[0]
    L = n // 2
    c = w3.shape[1] // 2
    tl = FILTER_TILE
    half = L // tl
    return pl.pallas_call(
        functools.partial(_filter_kernel, tl=tl, seq=L, c=c),
        grid=(n // tl,),
        in_specs=[pl.BlockSpec((tl, z2.shape[1]), lambda i: (i, 0))]
        + [_full(a) for a in (w1, b1, f1, w2, b2, f2)]
        + [pl.BlockSpec((w3.shape[0], c), lambda i: (0, i // half)), _full(deltas)],
        out_specs=pl.BlockSpec((tl, c), lambda i: (i, 0)),
        out_shape=jax.ShapeDtypeStruct((n, c), F32),
        compiler_params=_params("parallel"),
        name="hyena_filter",
    )(z2, w1, b1, f1, w2, b2, f2, w3, deltas)


def _dft_outer_kernel(u_ref, m_ref, ar_ref, ai_ref, *, n1):
    a = _dot(m_ref[...], u_ref[...].astype(BF16))
    ar_ref[...] = a[:n1].astype(ar_ref.dtype)
    ai_ref[...] = a[n1:].astype(ai_ref.dtype)


def _dft_outer(u2, m_outer, out_dtype):
    B, n1_in, W = u2.shape
    n1 = m_outer.shape[0] // 2
    lb = min(DFT_LANE_BLOCK, W)
    blk = lambda rows: pl.BlockSpec((None, rows, lb), lambda b, j: (b, 0, j))
    return pl.pallas_call(
        functools.partial(_dft_outer_kernel, n1=n1),
        grid=(B, W // lb),
        in_specs=[blk(n1_in), _full(m_outer)],
        out_specs=(blk(n1), blk(n1)),
        out_shape=(jax.ShapeDtypeStruct((B, n1, W), out_dtype), jax.ShapeDtypeStruct((B, n1, W), out_dtype)),
        compiler_params=_params("parallel", "parallel"),
        name="dft_outer",
    )(u2, m_outer)


def _twiddle(ar, ai, cph, sph, conj):
    if conj:
        return ar * cph - ai * sph, ai * cph + ar * sph
    return ar * cph + ai * sph, ai * cph - ar * sph


def _dft_spectrum_kernel(ar_ref, ai_ref, twc_ref, tws_ref, mf_ref, kf_ref, *, kb, n2, scale):
    for r in range(kb):
        pr, pi = _twiddle(ar_ref[r], ai_ref[r], twc_ref[r], tws_ref[r], conj=False)
        a = jnp.concatenate([pr.astype(BF16), pi.astype(BF16)], axis=0)
        kf_ref[r] = _dot(mf_ref[...], a) * scale


def _dft_spectrum(ar, ai, twc, tws, m_fwd, *, scale):
    n1, n2, c = ar.shape
    kb = DFT_K1_BLOCK
    blk = lambda rows, w: pl.BlockSpec((kb, rows, w), lambda i: (i, 0, 0))
    return pl.pallas_call(
        functools.partial(_dft_spectrum_kernel, kb=kb, n2=n2, scale=scale),
        grid=(n1 // kb,),
        in_specs=[blk(n2, c), blk(n2, c), blk(n2, 1), blk(n2, 1), _full(m_fwd)],
        out_specs=blk(2 * n2, c),
        out_shape=jax.ShapeDtypeStruct((n1, 2 * n2, c), F32),
        compiler_params=_params("parallel"),
        name="dft_filter_spectrum",
    )(ar, ai, twc, tws, m_fwd)


def _dft_inner_kernel(ar_ref, ai_ref, twc_ref, tws_ref, kf_ref, mf_ref, mi_ref, br_ref, bi_ref, *, kb, n2):
    for r in range(kb):
        cph = twc_ref[r]
        sph = tws_ref[r]
        pr, pi = _twiddle(ar_ref[r].astype(F32), ai_ref[r].astype(F32), cph, sph, conj=False)
        a = jnp.concatenate([pr.astype(BF16), pi.astype(BF16)], axis=0)
        x = _dot(mf_ref[...], a)
        xr, xi = x[:n2], x[n2:]
        kr, ki = kf_ref[r, :n2], kf_ref[r, n2:]
        y = jnp.concatenate([(xr * kr - xi * ki).astype(BF16), (xr * ki + xi * kr).astype(BF16)], axis=0)
        bb = _dot(mi_ref[...], y)
        qr, qi = _twiddle(bb[:n2], bb[n2:], cph, sph, conj=True)
        br_ref[r] = qr.astype(br_ref.dtype)
        bi_ref[r] = qi.astype(bi_ref.dtype)


def _dft_inner(ar, ai, twc, tws, kf, m_fwd, m_inv):
    B, n1, n2, c = ar.shape
    kb = DFT_K1_BLOCK
    sig = pl.BlockSpec((None, kb, n2, c), lambda i, b: (b, i, 0, 0))
    tab = lambda rows, w: pl.BlockSpec((kb, rows, w), lambda i, b: (i, 0, 0))
    return pl.pallas_call(
        functools.partial(_dft_inner_kernel, kb=kb, n2=n2),
        grid=(n1 // kb, B),
        in_specs=[sig, sig, tab(n2, 1), tab(n2, 1), tab(2 * n2, c), _full(m_fwd), _full(m_inv)],
        out_specs=(sig, sig),
        out_shape=(jax.ShapeDtypeStruct(ar.shape, BF16), jax.ShapeDtypeStruct(ar.shape, BF16)),
        compiler_params=_params("parallel", "parallel"),
        name="dft_inner",
    )(ar, ai, twc, tws, kf, m_fwd, m_inv)


def _dft_outer_inv_kernel(br_ref, bi_ref, m_ref, u_ref, x0_ref, d_ref, o_ref):
    b = jnp.concatenate([br_ref[...].astype(BF16), bi_ref[...].astype(BF16)], axis=0)
    y = _dot(m_ref[...], b)
    u = u_ref[...].astype(F32)
    o_ref[...] = (x0_ref[...].astype(F32) * (y + u * d_ref[...])).astype(o_ref.dtype)


def _dft_outer_inv(br, bi, m_inv_outer, u2, x02, d_lanes):
    B, n1, W = br.shape
    lb = d_lanes.shape[1]
    blk = lambda rows: pl.BlockSpec((None, rows, lb), lambda b, j: (b, 0, j))
    return pl.pallas_call(
        _dft_outer_inv_kernel,
        grid=(B, W // lb),
        in_specs=[blk(n1), blk(n1), _full(m_inv_outer), blk(n1 // 2), blk(n1 // 2), _full(d_lanes)],
        out_specs=blk(n1 // 2),
        out_shape=jax.ShapeDtypeStruct((B, n1 // 2, W), BF16),
        compiler_params=_params("parallel", "parallel"),
        name="dft_outer_inv_gate",
    )(br, bi, m_inv_outer, u2, x02, d_lanes)


@functools.lru_cache(maxsize=None)
def _dft_tables(seq):
    n = 2 * seq
    n2 = DFT_N2
    n1 = n // n2
    k1 = np.arange(n1)[:, None]
    th = 2.0 * np.pi * ((k1 * np.arange(n1)[None, :]) % n1) / n1
    c1, s1 = np.cos(th), np.sin(th)
    m_outer_full = np.concatenate([c1, -s1], axis=0)
    m_outer_half = m_outer_full[:, :n1 // 2]
    m_outer_inv = np.concatenate([c1[:n1 // 2], -s1[:n1 // 2]], axis=1)
    ps = 2.0 * np.pi * ((np.arange(n2)[:, None] * np.arange(n2)[None, :]) % n2) / n2
    c2, s2 = np.cos(ps), np.sin(ps)
    m_fwd = np.block([[c2, s2], [-s2, c2]])
    m_inv = np.block([[c2, -s2], [s2, c2]])
    ph = 2.0 * np.pi * (k1 * np.arange(n2)[None, :]) / n
    f = lambda a: np.asarray(a, np.float32)
    return dict(n1=n1, n2=n2, m_outer_full=f(m_outer_full), m_outer_half=f(m_outer_half),
                m_outer_inv=f(m_outer_inv), m_fwd=f(m_fwd), m_inv=f(m_inv),
                twc=f(np.cos(ph))[:, :, None], tws=f(np.sin(ph))[:, :, None])


def _filter_features(seq):
    t = jnp.linspace(0.0, 1.0, seq, dtype=F32)[:, None]
    w = 2.0 * math.pi * jnp.arange(seq, dtype=F32) / seq
    f = jnp.linspace(1e-4, FILTER_BANDS - 1, FILTER_BANDS, dtype=F32)
    ang = w[:, None] * f[None, :]
    z = jnp.concatenate([t, jnp.cos(ang), -jnp.sin(ang)], axis=-1)
    z = jnp.concatenate([z, z[seq - 1:], z[:0:-1]], axis=0)
    return jnp.pad(z, ((0, 0), (0, LANES - FILTER_EMB)))


def _hyena(phy, lp):
    B, L, W = phy.shape
    c = W // 3
    tb = _dft_tables(L)
    n1, n2 = tb["n1"], tb["n2"]
    bf = lambda name: jnp.asarray(tb[name]).astype(BF16)
    twc, tws = jnp.asarray(tb["twc"]), jnp.asarray(tb["tws"])

    u, x0 = _hygate(phy, lp["conv_w"], lp["conv_b"])

    kern = _filters(_filter_features(L), lp["filt_w1"], lp["filt_b1"], lp["filt_f1"], lp["filt_w2"],
                    lp["filt_b2"], lp["filt_f2"], lp["filt_w3"], lp["deltas"])
    far, fai = _dft_outer(kern.reshape(1, n1, n2 * c), bf("m_outer_full"), F32)
    kf = _dft_spectrum(far.reshape(n1, n2, c), fai.reshape(n1, n2, c), twc, tws, bf("m_fwd"),
                       scale=1.0 / (2 * L))

    u2 = u.reshape(B, n1 // 2, n2 * c)
    ar, ai = _dft_outer(u2, bf("m_outer_half"), BF16)
    br, bi = _dft_inner(ar.reshape(B, n1, n2, c), ai.reshape(B, n1, n2, c), twc, tws, kf,
                        bf("m_fwd"), bf("m_inv"))
    lb = min(DFT_LANE_BLOCK, n2 * c)
    d_lanes = jnp.tile(lp["hyena_d"], (1, lb // c))
    hyo = _dft_outer_inv(br.reshape(B, n1, n2 * c), bi.reshape(B, n1, n2 * c), bf("m_outer_inv"),
                         u2, x0.reshape(B, n1 // 2, n2 * c), d_lanes)
    return hyo.reshape(B, L, c)


def _outproj_kernel(att_ref, hyo_ref, x_ref, wa_ref, wh_ref, bo_ref, g_ref, b_ref, wr_hi_ref, wr_lo_ref,
                    br_ref, x1_ref, comb_ref, *, alpha):
    m = _dot(att_ref[...], wa_ref[...]) + _dot(hyo_ref[...], wh_ref[...]) + bo_ref[...]
    x1 = _layer_norm(alpha * x_ref[...] + m, g_ref[...], b_ref[...])
    x1_ref[...] = x1

    xh, xl = _split_bf16(x1)
    lg = _dot(xh, wr_hi_ref[...]) + (_dot(xh, wr_lo_ref[...]) + _dot(xl, wr_hi_ref[...])) + br_ref[...]
    lane = lax.broadcasted_iota(jnp.int32, lg.shape, 1)
    big = jnp.int32(ROUTER_LANES)
    ninf = jnp.float32(-jnp.inf)
    first = lambda mask: jnp.min(jnp.where(mask, lane, big), axis=-1, keepdims=True)

    gmask = lane < N_GROUPS
    gmax = jnp.max(jnp.where(gmask, lg, ninf), axis=-1, keepdims=True)
    gsum = jnp.sum(jnp.where(gmask, jnp.exp(lg - gmax), 0.0), axis=-1, keepdims=True)
    g_w = 1.0 / gsum
    g_idx = first(gmask & (lg == gmax))
    lo = N_GROUPS + EXPERTS_PER_GROUP * g_idx
    emask = (lane >= lo) & (lane < lo + EXPERTS_PER_GROUP)
    e1 = jnp.max(jnp.where(emask, lg, ninf), axis=-1, keepdims=True)
    i1 = first(emask & (lg == e1))
    rest = emask & (lane != i1)
    e2 = jnp.max(jnp.where(rest, lg, ninf), axis=-1, keepdims=True)
    i2 = first(rest & (lg == e2))
    tt = jnp.exp(e2 - e1)
    w1 = g_w / (1.0 + tt)
    comb_ref[...] = (jnp.where(lane == i1, w1, 0.0) + jnp.where(lane == i2, w1 * tt, 0.0)
                     + jnp.where(lane == 0, g_idx.astype(F32), 0.0))


def _outproj_ln_router(att, hyo, x, wa, wh, bo, g, b, wr_hi, wr_lo, br, *, alpha):
    B, L, D = x.shape
    tm = PROJ_TILE
    tok = lambda w: pl.BlockSpec((None, tm, w), lambda bb, i: (bb, i, 0))
    return pl.pallas_call(
        functools.partial(_outproj_kernel, alpha=alpha),
        grid=(B, L // tm),
        in_specs=[tok(att.shape[2]), tok(hyo.shape[2]), tok(D)]
        + [_full(a) for a in (wa, wh, bo, g, b, wr_hi, wr_lo, br)],
        out_specs=(tok(D), tok(ROUTER_LANES)),
        out_shape=(jax.ShapeDtypeStruct((B, L, D), F32), jax.ShapeDtypeStruct((B, L, ROUTER_LANES), F32)),
        compiler_params=_params("parallel", "parallel"),
        name="outproj_ln_router",
    )(att, hyo, x, wa, wh, bo, g, b, wr_hi, wr_lo, br)


def _moe_kernel(meta_ref, x_ref, comb_ref, w1_ref, w3_ref, w2_ref, g_ref, b_ref, o_ref,
                xs_ref, ys_ref, cs_ref, pt_ref, *, alpha, tb, slots, tile):
    bi = pl.program_id(0)
    blk = pl.program_id(1)
    step = pl.program_id(2)
    grp = step // (EXPERTS_PER_GROUP // MOE_EXPERTS_PER_STEP)

    @pl.when(step == 0)
    def _():
        comb = comb_ref[...]
        starts = [meta_ref[bi, blk, g].astype(F32) for g in range(N_GROUPS)]
        gid_row = comb.T[0:1, :]
        grow = lax.broadcasted_iota(jnp.int32, (8, tb), 0).astype(F32)
        oh_row = jnp.where(gid_row == grow, 1.0, 0.0)
        r_i = lax.broadcasted_iota(jnp.int32, (tb, tb), 0)
        c_i = lax.broadcasted_iota(jnp.int32, (tb, tb), 1)
        earlier_row = jnp.where(r_i < c_i, 1.0, 0.0).astype(BF16)
        rank_row = jnp.sum(oh_row * _dot(oh_row.astype(BF16), earlier_row), axis=0, keepdims=True)
        pos_row = rank_row
        for g in range(N_GROUPS):
            pos_row = pos_row + oh_row[g:g + 1] * starts[g]
        slot_r = lax.broadcasted_iota(jnp.int32, (slots, tb), 0).astype(F32)
        p = jnp.where(slot_r == pos_row, 1.0, 0.0).astype(BF16)
        xs_ref[...] = _dot(p, x_ref[...].astype(BF16)).astype(BF16)
        ch, cl = _split_bf16(comb)
        cs_ref[...] = _dot(p, ch) + _dot(p, cl)

        gid_col = comb[:, 0:1]
        glane = lax.broadcasted_iota(jnp.int32, comb.shape, 1).astype(F32)
        oh_col = jnp.where(gid_col == glane, 1.0, 0.0)
        earlier_col = jnp.where(c_i < r_i, 1.0, 0.0).astype(BF16)
        rank_col = jnp.sum(oh_col * _dot(earlier_col, oh_col.astype(BF16)), axis=1, keepdims=True)
        pos_col = rank_col
        for g in range(N_GROUPS):
            pos_col = pos_col + oh_col[:, g:g + 1] * starts[g]
        slot_c = lax.broadcasted_iota(jnp.int32, (tb, slots), 1).astype(F32)
        pt_ref[...] = jnp.where(slot_c == pos_col, 1.0, 0.0).astype(BF16)
        ys_ref[...] = jnp.zeros_like(ys_ref)

    start = meta_ref[bi, blk, grp]
    n_tiles = meta_ref[bi, blk, N_GROUPS + grp]

    def row_tile(i, carry):
        rows = pl.ds(pl.multiple_of(start + i * tile, MOE_GROUP_ALIGN), tile)
        xt = xs_ref[rows, :]
        c = cs_ref[rows, :]
        lane = lax.broadcasted_iota(jnp.int32, c.shape, 1)
        hid = [jax.nn.silu(_dot(xt, w1_ref[k])) * _dot(xt, w3_ref[k]) for k in range(MOE_EXPERTS_PER_STEP)]
        y = None
        for k in range(MOE_EXPERTS_PER_STEP):
            e = N_GROUPS + step * MOE_EXPERTS_PER_STEP + k
            ce = jnp.sum(jnp.where(lane == e, c, 0.0), axis=-1, keepdims=True)
            yk = ce * _dot(hid[k].astype(BF16), w2_ref[k])
            y = yk if y is None else y + yk
        ys_ref[rows, :] += y
        return carry

    lax.fori_loop(0, n_tiles, row_tile, 0)

    @pl.when(step == pl.num_programs(2) - 1)
    def _():
        f = _dot(pt_ref[...], ys_ref[...].astype(BF16))
        o_ref[...] = _layer_norm(alpha * x_ref[...] + f, g_ref[...], b_ref[...])


def _moe_ln(x1, comb, w1, w3, w2, g, b, *, alpha):
    B, L, D = x1.shape
    E, _, de = w1.shape
    tb = min(MOE_BLOCK, L)
    tile = MOE_ROW_TILE
    per = MOE_EXPERTS_PER_STEP
    nblk = L // tb
    slots = tb + 2 * tile
    gid = comb[..., 0].astype(jnp.int32).reshape(B, nblk, tb)
    cnt = jnp.sum(gid[..., None] == jnp.arange(N_GROUPS), axis=2).astype(jnp.int32)
    padded = (cnt + MOE_GROUP_ALIGN - 1) // MOE_GROUP_ALIGN * MOE_GROUP_ALIGN
    meta = jnp.concatenate([jnp.cumsum(padded, axis=-1) - padded, (cnt + tile - 1) // tile], axis=-1)

    tok = lambda w: pl.BlockSpec((None, tb, w), lambda bb, i, e, m: (bb, i, 0))
    const = lambda a: pl.BlockSpec(a.shape, lambda bb, i, e, m: (0,) * a.ndim)
    return pl.pallas_call(
        functools.partial(_moe_kernel, alpha=alpha, tb=tb, slots=slots, tile=tile),
        grid_spec=pltpu.PrefetchScalarGridSpec(
            num_scalar_prefetch=1,
            grid=(B, nblk, E // per),
            in_specs=[tok(D), tok(ROUTER_LANES),
                      pl.BlockSpec((per, D, de), lambda bb, i, s, m: (s, 0, 0)),
                      pl.BlockSpec((per, D, de), lambda bb, i, s, m: (s, 0, 0)),
                      pl.BlockSpec((per, de, D), lambda bb, i, s, m: (s, 0, 0)),
                      const(g), const(b)],
            out_specs=tok(D),
            scratch_shapes=[pltpu.VMEM((slots, D), BF16), pltpu.VMEM((slots, D), F32),
                            pltpu.VMEM((slots, ROUTER_LANES), F32), pltpu.VMEM((tb, slots), BF16)]),
        out_shape=jax.ShapeDtypeStruct((B, L, D), F32),
        compiler_params=_params("parallel", "parallel", "arbitrary"),
        name="moe_ln",
    )(meta, x1, comb, w1, w3, w2, g, b)


def _alibi_tables():
    t = ATT_TILE
    slopes = jnp.asarray(LOG2E * 2.0 ** (-8.0 * np.arange(1, ATT_HEADS + 1) / ATT_HEADS), dtype=F32)
    off = (jnp.arange(t)[:, None] - jnp.arange(t)[None, :]).astype(F32)
    dpos = slopes[:, None, None] * off[None]
    return slopes, jnp.stack([dpos, -jnp.abs(dpos), -dpos], axis=1)


def _prep_layer(l, depth, w_in, b_in, conv_w, conv_b, lam_q1, lam_k1, lam_q2, lam_k2, subln_g,
                filt_w1, filt_b1, filt_freq1, filt_w2, filt_b2, filt_freq2, filt_w3, hyena_d,
                w_out, b_out, ln1_g, ln1_b, router_group_w, router_group_b, router_expert_w,
                router_expert_b, exp_w1, exp_w3, exp_w2, ln2_g, ln2_b):
    a0, a1, a2 = ATT_QK_WIDTH, 2 * ATT_QK_WIDTH, 2 * ATT_QK_WIDTH + ATT_WIDTH
    scale = ATT_HEAD_DIM ** -0.5 * LOG2E
    row = lambda v: v.reshape(1, -1)
    w, b = w_in[l], b_in[l]
    c = hyena_d.shape[1]
    wr = jnp.concatenate([router_group_w[l], router_expert_w[l]], axis=1)
    wr = jnp.pad(wr, ((0, 0), (0, ROUTER_LANES - wr.shape[1])))
    wr_hi = wr.astype(BF16)
    br = jnp.concatenate([router_group_b[l], router_expert_b[l]])
    max_decay = abs(math.log(DECAY_TARGET) / FAST_DECAY_PCT)
    min_decay = abs(math.log(DECAY_TARGET) / SLOW_DECAY_PCT)
    return dict(
        lam_init=0.8 - 0.6 * math.exp(-0.3 * l),
        alpha=(2.0 * depth) ** 0.25,
        wq=(w[:, :a0] * scale).astype(BF16), bq=row(b[:a0] * scale),
        wk=w[:, a0:a1].astype(BF16), bk=row(b[a0:a1]),
        wvt=w[:, a1:a2].T.astype(BF16), bvt=b[a1:a2].reshape(-1, 1),
        why=w[:, a2:].astype(BF16), bhy=row(b[a2:]),
        conv_w=conv_w[l], conv_b=row(conv_b[l]),
        lamv=jnp.stack([lam_q1[l], lam_k1[l], lam_q2[l], lam_k2[l]]).astype(F32),
        subln_g=row(subln_g[l]),
        filt_w1=jnp.pad(filt_w1[l], ((0, LANES - FILTER_EMB), (0, 0))), filt_b1=row(filt_b1[l]),
        filt_f1=row(filt_freq1[l]), filt_w2=filt_w2[l], filt_b2=row(filt_b2[l]), filt_f2=row(filt_freq2[l]),
        filt_w3=filt_w3[l],
        deltas=row(jnp.linspace(min_decay, max_decay, c, dtype=F32)),
        hyena_d=row(hyena_d[l]),
        wa=w_out[l][:ATT_WIDTH].astype(BF16), wh=w_out[l][ATT_WIDTH:].astype(BF16), bo=row(b_out[l]),
        ln1_g=row(ln1_g[l]), ln1_b=row(ln1_b[l]),
        wr_hi=wr_hi, wr_lo=(wr - wr_hi.astype(F32)).astype(BF16),
        br=row(jnp.pad(br, (0, ROUTER_LANES - br.shape[0]))),
        w1=exp_w1[l].astype(BF16), w3=exp_w3[l].astype(BF16), w2=exp_w2[l].astype(BF16),
        ln2_g=row(ln2_g[l]), ln2_b=row(ln2_b[l]),
    )


def _layer(x, lp, alibi):
    slopes, dist = alibi
    q, k, vt, phy = _inproj(x, lp["wq"], lp["bq"], lp["wk"], lp["bk"], lp["wvt"], lp["bvt"],
                            lp["why"], lp["bhy"])
    att = _attention(q, k, vt, slopes, dist, lp["lamv"], lp["subln_g"],
                     lam_init=lp["lam_init"])
    hyo = _hyena(phy, lp)
    x1, comb = _outproj_ln_router(att, hyo, x, lp["wa"], lp["wh"], lp["bo"], lp["ln1_g"], lp["ln1_b"],
                                  lp["wr_hi"], lp["wr_lo"], lp["br"], alpha=lp["alpha"])
    return _moe_ln(x1, comb, lp["w1"], lp["w3"], lp["w2"], lp["ln2_g"], lp["ln2_b"], alpha=lp["alpha"])


def kernel(x_prompt, x_sample, w_in, b_in, conv_w, conv_b, lam_q1, lam_k1, lam_q2, lam_k2, subln_g, filt_w1, filt_b1, filt_freq1, filt_w2, filt_b2, filt_freq2, filt_w3, hyena_d, w_out, b_out, ln1_g, ln1_b, router_group_w, router_group_b, router_expert_w, router_expert_b, exp_w1, exp_w3, exp_w2, ln2_g, ln2_b):
    params = (w_in, b_in, conv_w, conv_b, lam_q1, lam_k1, lam_q2, lam_k2, subln_g, filt_w1, filt_b1,
              filt_freq1, filt_w2, filt_b2, filt_freq2, filt_w3, hyena_d, w_out, b_out, ln1_g, ln1_b,
              router_group_w, router_group_b, router_expert_w, router_expert_b, exp_w1, exp_w3, exp_w2,
              ln2_g, ln2_b)
    depth = w_in.shape[0]
    layers = [_prep_layer(l, depth, *params) for l in range(depth)]
    alibi = _alibi_tables()

    def trunk(x):
        for lp in layers:
            x = _layer(x, lp, alibi)
        return x

    return (trunk(x_prompt), trunk(x_sample))
```

```python
import functools
import math

import numpy as np
import jax
import jax.numpy as jnp
from jax import lax
from jax.experimental import pallas as pl
from jax.experimental.pallas import tpu as pltpu

F32 = jnp.float32
BF16 = jnp.bfloat16

ATT_HEADS = 4
ATT_HEAD_DIM = 64
ATT_V_DIM = 2 * ATT_HEAD_DIM
ATT_QK_WIDTH = ATT_HEADS * 2 * ATT_HEAD_DIM
ATT_WIDTH = ATT_HEADS * ATT_V_DIM
FILTER_EMB = 33
FILTER_BANDS = (FILTER_EMB - 1) // 2
DECAY_TARGET = 1e-2
FAST_DECAY_PCT = 0.3
SLOW_DECAY_PCT = 1.5
FILTER_SHIFT = 0.05
N_GROUPS = 4
EXPERTS_PER_GROUP = 4
N_EXPERTS = N_GROUPS * EXPERTS_PER_GROUP
LN_EPS = 1e-5
RMS_EPS = 1e-5

LANES = 128
BF16_SUBLANES = 16
VMEM_LIMIT_BYTES = 56 * 1024 * 1024

ATT_TILE = 256
ATT_Q_TILES = 1
ATT_KEY_BLOCK = 1024
ATT_LOOP_BLOCKS = 4
VT_ROWS = ATT_V_DIM + BF16_SUBLANES
PROJ_TILE = 1024
GATE_TILE = 1024
FILTER_TILE = 512
DFT_N2 = 128
DFT_LANE_BLOCK = 4096
DFT_K1_BLOCK = 16
MOE_BLOCK = 1024
MOE_ROW_TILE = 128
MOE_GROUP_ALIGN = BF16_SUBLANES
MOE_EXPERTS_PER_STEP = 2
ROUTER_LANES = LANES
NEG_BIG = -1e30
LOG2E = math.log2(math.e)


def _params(*sem):
    return pltpu.CompilerParams(dimension_semantics=sem, vmem_limit_bytes=VMEM_LIMIT_BYTES)


def _full(a):
    nd = a.ndim
    return pl.BlockSpec(a.shape, lambda *_: (0,) * nd)


def _dot(a, b):
    return jnp.dot(a, b, preferred_element_type=F32)


def _dot_nt(a, b):
    return lax.dot_general(a, b, (((1,), (1,)), ((), ())), preferred_element_type=F32)


def _split_bf16(x):
    hi = x.astype(BF16)
    lo = (x - hi.astype(F32)).astype(BF16)
    return hi, lo


def _dot_3pass(a, b):
    ah, al = _split_bf16(a)
    bh, bl = _split_bf16(b)
    return _dot(ah, bh) + (_dot(ah, bl) + _dot(al, bh))


def _layer_norm(r, g, b):
    mu = jnp.mean(r, axis=-1, keepdims=True)
    c = r - mu
    var = jnp.mean(c * c, axis=-1, keepdims=True)
    return c * lax.rsqrt(var + LN_EPS) * g + b


def _inproj_kernel(x_ref, wq_ref, bq_ref, wk_ref, bk_ref, wvt_ref, bvt_ref, why_ref, bhy_ref,
                   q_ref, k_ref, vt_ref, phy_ref):
    xb = x_ref[...].astype(BF16)
    tm = xb.shape[0]
    q_ref[...] = (_dot(xb, wq_ref[...]) + bq_ref[...]).astype(BF16)
    k_ref[...] = (_dot(xb, wk_ref[...]) + bk_ref[...]).astype(BF16)
    vt = (_dot_nt(wvt_ref[...], xb) + bvt_ref[...]).astype(BF16)
    row = lax.broadcasted_iota(jnp.int32, (BF16_SUBLANES, tm), 0)
    ones_rows = jnp.where(row == 0, 1.0, 0.0).astype(BF16)
    for h in range(ATT_HEADS):
        vt_ref[h, :ATT_V_DIM, :] = vt[h * ATT_V_DIM:(h + 1) * ATT_V_DIM]
        vt_ref[h, ATT_V_DIM:, :] = ones_rows
    phy_ref[...] = (_dot(xb, why_ref[...]) + bhy_ref[...]).astype(phy_ref.dtype)


def _inproj(x, wq, bq, wk, bk, wvt, bvt, why, bhy):
    B, L, D = x.shape
    tm, kb = PROJ_TILE, min(ATT_KEY_BLOCK, L)
    per = kb // tm
    hyw = why.shape[1]
    tok = lambda w: pl.BlockSpec((None, tm, w), lambda b, i: (b, i, 0))
    return pl.pallas_call(
        _inproj_kernel,
        grid=(B, L // tm),
        in_specs=[tok(D)] + [_full(a) for a in (wq, bq, wk, bk, wvt, bvt, why, bhy)],
        out_specs=(tok(ATT_QK_WIDTH), tok(ATT_QK_WIDTH),
                   pl.BlockSpec((None, None, ATT_HEADS, VT_ROWS, tm),
                                lambda b, i: (b, i // per, 0, 0, i % per)),
                   tok(hyw)),
        out_shape=(jax.ShapeDtypeStruct((B, L, ATT_QK_WIDTH), BF16),
                   jax.ShapeDtypeStruct((B, L, ATT_QK_WIDTH), BF16),
                   jax.ShapeDtypeStruct((B, L // kb, ATT_HEADS, VT_ROWS, kb), BF16),
                   jax.ShapeDtypeStruct((B, L, hyw), BF16)),
        compiler_params=_params("parallel", "parallel"),
        name="inproj",
    )(x, wq, bq, wk, bk, wvt, bvt, why, bhy)


def _attn_kernel(slopes_ref, q_ref, k_ref, vt_ref, dist_ref, lam_ref, g_ref,
                 o_ref, acc_ref, s_ref, *, t, nq, kb, nkb, lam_init):
    h = pl.program_id(1)
    q0 = pl.program_id(2) * nq
    ns = kb // t
    slope = slopes_ref[h]
    q = q_ref[...]
    lane = lax.broadcasted_iota(jnp.int32, q.shape, 1)
    zero = jnp.zeros_like(q)
    qmaps = (jnp.where(lane < ATT_HEAD_DIM, q, zero), jnp.where(lane >= ATT_HEAD_DIM, q, zero))
    acc_ref[...] = jnp.zeros_like(acc_ref)

    def produce(qt, j, slot):
        kblk = k_ref[pl.ds(pl.multiple_of(j * kb, kb), kb), :]
        for mi in range(2):
            s = _dot_nt(kblk, qmaps[mi][qt * t:(qt + 1) * t])
            for st in range(ns):
                sel = jnp.clip(j * ns + st - (q0 + qt), -1, 1) + 1
                s_ref[qt, slot, mi, st * t:(st + 1) * t, :] = s[st * t:(st + 1) * t] + dist_ref[sel]

    def consume(qt, j, slot, ms):
        vt = vt_ref[j]
        cs = [-slope * (jnp.abs(j * ns + st - (q0 + qt)) * t).astype(F32) for st in range(ns)]
        out = []
        for mi in range(2):
            sub = [s_ref[qt, slot, mi, st * t:(st + 1) * t, :] for st in range(ns)]
            m_new = ms[mi]
            for st in range(ns):
                m_new = jnp.maximum(m_new, jnp.max(sub[st], axis=0, keepdims=True) + cs[st])
            e = jnp.concatenate([jnp.exp2(sub[st] - (m_new - cs[st])).astype(BF16) for st in range(ns)],
                                axis=0)
            acc_ref[qt, mi] = acc_ref[qt, mi] * jnp.exp2(ms[mi] - m_new) + _dot(vt, e)
            out.append(m_new)
        return tuple(out)

    def advance(j_next, slot_next, j, slot, ms):
        if j_next is not None:
            for qt in range(nq):
                produce(qt, j_next, slot_next)
        return tuple(consume(qt, j, slot, ms[qt]) for qt in range(nq))

    m0 = jnp.full((1, t), NEG_BIG, F32)
    ms = ((m0, m0),) * nq
    for qt in range(nq):
        produce(qt, 0, 0)
    def run(first, count, last, ms):
        for b in range(count):
            j = first + b
            nxt = None if (last and b == count - 1) else j + 1
            ms = advance(nxt, None if nxt is None else (b + 1) % 2, j, b % 2, ms)
        return ms

    per = ATT_LOOP_BLOCKS if nkb >= 2 * ATT_LOOP_BLOCKS else min(2, nkb)
    assert per % 2 == 0 or nkb == 1
    assert nkb % per == 0
    trips = nkb // per - 1
    if trips > 0:
        trips = jnp.minimum(pl.program_id(0) + trips, trips)
        ms = lax.fori_loop(0, trips, lambda i, ms: run(i * per, per, False, ms), ms)
    run(nkb - per, per, True, ms)

    lv = lam_ref[...]
    lam = (jnp.exp(jnp.sum(lv[0:1] * lv[1:2], axis=-1, keepdims=True))
           - jnp.exp(jnp.sum(lv[2:3] * lv[3:4], axis=-1, keepdims=True)) + lam_init)
    for qt in range(nq):
        a0 = acc_ref[qt, 0]
        a1 = acc_ref[qt, 1]
        o0 = a0[:ATT_V_DIM] / a0[ATT_V_DIM:ATT_V_DIM + 1]
        o1 = a1[:ATT_V_DIM] / a1[ATT_V_DIM:ATT_V_DIM + 1]
        d = o0 - lam * o1
        d = d * lax.rsqrt(jnp.mean(d * d, axis=0, keepdims=True) + RMS_EPS)
        o_ref[qt * t:(qt + 1) * t, :] = (d.T * (g_ref[...] * (1.0 - lam_init))).astype(o_ref.dtype)


def _attention(q, k, vt, slopes, dist, lamv, g, *, lam_init):
    B, L, _ = q.shape
    t = ATT_TILE
    nq = min(ATT_Q_TILES, L // t)
    nkb, kb = vt.shape[1], vt.shape[4]
    assert nkb == 1 or nkb % 2 == 0
    return pl.pallas_call(
        functools.partial(_attn_kernel, t=t, nq=nq, kb=kb, nkb=nkb, lam_init=lam_init),
        grid=(B, ATT_HEADS, L // (nq * t)),
        in_specs=[pl.BlockSpec(memory_space=pltpu.SMEM),
                  pl.BlockSpec((None, nq * t, LANES), lambda b, h, i: (b, i, h)),
                  pl.BlockSpec((None, L, LANES), lambda b, h, i: (b, 0, h)),
                  pl.BlockSpec((None, nkb, None, VT_ROWS, kb), lambda b, h, i: (b, 0, h, 0, 0)),
                  pl.BlockSpec((None, 3, t, t), lambda b, h, i: (h, 0, 0, 0)),
                  _full(lamv), _full(g)],
        out_specs=pl.BlockSpec((None, nq * t, LANES), lambda b, h, i: (b, i, h)),
        out_shape=jax.ShapeDtypeStruct((B, L, ATT_WIDTH), BF16),
        scratch_shapes=[pltpu.VMEM((nq, 2, VT_ROWS, t), F32), pltpu.VMEM((nq, 2, 2, kb, t), F32)],
        compiler_params=_params("parallel", "parallel", "arbitrary"),
        name="diff_attention",
    )(slopes, q, k, vt, dist, lamv, g)


def _hygate_kernel(p_ref, prev_ref, next_ref, w_ref, b_ref, u_ref, x0_ref, *, tl, c):
    i = pl.program_id(1)
    last = pl.num_programs(1) - 1
    x = p_ref[...].astype(F32)
    sub = prev_ref.shape[0]
    prev_row = jnp.where(i == 0, 0.0, prev_ref[...].astype(F32)[sub - 1:sub, :])
    next_row = jnp.where(i == last, 0.0, next_ref[...].astype(F32)[0:1, :])
    row = lax.broadcasted_iota(jnp.int32, x.shape, 0)
    xm = jnp.where(row == 0, prev_row, pltpu.roll(x, 1, 0))
    xp = jnp.where(row == tl - 1, next_row, pltpu.roll(x, tl - 1, 0))
    w = w_ref[...]
    hy = xm * w[0:1] + x * w[1:2] + xp * w[2:3] + b_ref[...]
    x0_ref[...] = hy[:, :c].astype(x0_ref.dtype)
    u_ref[...] = (hy[:, 2 * c:] * hy[:, c:2 * c]).astype(u_ref.dtype)


def _hygate(phy, conv_w, conv_b):
    B, L, W = phy.shape
    c = W // 3
    tl = GATE_TILE
    sub = BF16_SUBLANES
    nb = tl // sub
    last_blk = L // sub - 1
    return pl.pallas_call(
        functools.partial(_hygate_kernel, tl=tl, c=c),
        grid=(B, L // tl),
        in_specs=[pl.BlockSpec((None, tl, W), lambda b, i: (b, i, 0)),
                  pl.BlockSpec((None, sub, W), lambda b, i: (b, jnp.maximum(i * nb - 1, 0), 0)),
                  pl.BlockSpec((None, sub, W), lambda b, i: (b, jnp.minimum((i + 1) * nb, last_blk), 0)),
                  _full(conv_w), _full(conv_b)],
        out_specs=(pl.BlockSpec((None, tl, c), lambda b, i: (b, i, 0)),
                   pl.BlockSpec((None, tl, c), lambda b, i: (b, i, 0))),
        out_shape=(jax.ShapeDtypeStruct((B, L, c), BF16), jax.ShapeDtypeStruct((B, L, c), BF16)),
        compiler_params=_params("parallel", "parallel"),
        name="hyena_gate",
    )(phy, phy, phy, conv_w, conv_b)


def _filter_kernel(z_ref, w1_ref, b1_ref, f1_ref, w2_ref, b2_ref, f2_ref, w3_ref, deltas_ref,
                   k_ref, *, tl, seq, c):
    i = pl.program_id(0)
    h = jnp.sin(f1_ref[...] * (_dot_3pass(z_ref[...], w1_ref[...]) + b1_ref[...]))
    h = jnp.sin(f2_ref[...] * (_dot_3pass(h, w2_ref[...]) + b2_ref[...]))
    h = _dot_3pass(h, w3_ref[...])
    n = lax.broadcasted_iota(jnp.int32, (tl, c), 0) + i * tl
    lag = jnp.where(n < seq, n, 2 * seq - n).astype(F32)
    decay = jnp.exp(-(lag * (1.0 / (seq - 1))) * deltas_ref[...]) + FILTER_SHIFT
    k_ref[...] = jnp.where(n == seq, 0.0, h * decay)


def _filters(z2, w1, b1, f1, w2, b2, f2, w3, deltas):
    n = z2.shape[0]
    L = n // 2
    c = w3.shape[1] // 2
    tl = FILTER_TILE
    half = L // tl
    return pl.pallas_call(
        functools.partial(_filter_kernel, tl=tl, seq=L, c=c),
        grid=(n // tl,),
        in_specs=[pl.BlockSpec((tl, z2.shape[1]), lambda i: (i, 0))]
        + [_full(a) for a in (w1, b1, f1, w2, b2, f2)]
        + [pl.BlockSpec((w3.shape[0], c), lambda i: (0, i // half)), _full(deltas)],
        out_specs=pl.BlockSpec((tl, c), lambda i: (i, 0)),
        out_shape=jax.ShapeDtypeStruct((n, c), F32),
        compiler_params=_params("parallel"),
        name="hyena_filter",
    )(z2, w1, b1, f1, w2, b2, f2, w3, deltas)


def _dft_outer_kernel(u_ref, m_ref, ar_ref, ai_ref, *, n1):
    a = _dot(m_ref[...], u_ref[...].astype(BF16))
    ar_ref[...] = a[:n1].astype(ar_ref.dtype)
    ai_ref[...] = a[n1:].astype(ai_ref.dtype)


def _dft_outer(u2, m_outer, out_dtype):
    B, n1_in, W = u2.shape
    n1 = m_outer.shape[0] // 2
    lb = min(DFT_LANE_BLOCK, W)
    blk = lambda rows: pl.BlockSpec((None, rows, lb), lambda b, j: (b, 0, j))
    return pl.pallas_call(
        functools.partial(_dft_outer_kernel, n1=n1),
        grid=(B, W // lb),
        in_specs=[blk(n1_in), _full(m_outer)],
        out_specs=(blk(n1), blk(n1)),
        out_shape=(jax.ShapeDtypeStruct((B, n1, W), out_dtype), jax.ShapeDtypeStruct((B, n1, W), out_dtype)),
        compiler_params=_params("parallel", "parallel"),
        name="dft_outer",
    )(u2, m_outer)


def _twiddle(ar, ai, cph, sph, conj):
    if conj:
        return ar * cph - ai * sph, ai * cph + ar * sph
    return ar * cph + ai * sph, ai * cph - ar * sph


def _dft_spectrum_kernel(ar_ref, ai_ref, twc_ref, tws_ref, mf_ref, kf_ref, *, kb, n2, scale):
    for r in range(kb):
        pr, pi = _twiddle(ar_ref[r], ai_ref[r], twc_ref[r], tws_ref[r], conj=False)
        a = jnp.concatenate([pr.astype(BF16), pi.astype(BF16)], axis=0)
        kf_ref[r] = _dot(mf_ref[...], a) * scale


def _dft_spectrum(ar, ai, twc, tws, m_fwd, *, scale):
    n1, n2, c = ar.shape
    kb = DFT_K1_BLOCK
    blk = lambda rows, w: pl.BlockSpec((kb, rows, w), lambda i: (i, 0, 0))
    return pl.pallas_call(
        functools.partial(_dft_spectrum_kernel, kb=kb, n2=n2, scale=scale),
        grid=(n1 // kb,),
        in_specs=[blk(n2, c), blk(n2, c), blk(n2, 1), blk(n2, 1), _full(m_fwd)],
        out_specs=blk(2 * n2, c),
        out_shape=jax.ShapeDtypeStruct((n1, 2 * n2, c), F32),
        compiler_params=_params("parallel"),
        name="dft_filter_spectrum",
    )(ar, ai, twc, tws, m_fwd)


def _dft_inner_kernel(ar_ref, ai_ref, twc_ref, tws_ref, kf_ref, mf_ref, mi_ref, br_ref, bi_ref, *, kb, n2):
    for r in range(kb):
        cph = twc_ref[r]
        sph = tws_ref[r]
        pr, pi = _twiddle(ar_ref[r].astype(F32), ai_ref[r].astype(F32), cph, sph, conj=False)
        a = jnp.concatenate([pr.astype(BF16), pi.astype(BF16)], axis=0)
        x = _dot(mf_ref[...], a)
        xr, xi = x[:n2], x[n2:]
        kr, ki = kf_ref[r, :n2], kf_ref[r, n2:]
        y = jnp.concatenate([(xr * kr - xi * ki).astype(BF16), (xr * ki + xi * kr).astype(BF16)], axis=0)
        bb = _dot(mi_ref[...], y)
        qr, qi = _twiddle(bb[:n2], bb[n2:], cph, sph, conj=True)
        br_ref[r] = qr.astype(br_ref.dtype)
        bi_ref[r] = qi.astype(bi_ref.dtype)


def _dft_inner(ar, ai, twc, tws, kf, m_fwd, m_inv):
    B, n1, n2, c = ar.shape
    kb = DFT_K1_BLOCK
    sig = pl.BlockSpec((None, kb, n2, c), lambda i, b: (b, i, 0, 0))
    tab = lambda rows, w: pl.BlockSpec((kb, rows, w), lambda i, b: (i, 0, 0))
    return pl.pallas_call(
        functools.partial(_dft_inner_kernel, kb=kb, n2=n2),
        grid=(n1 // kb, B),
        in_specs=[sig, sig, tab(n2, 1), tab(n2, 1), tab(2 * n2, c), _full(m_fwd), _full(m_inv)],
        out_specs=(sig, sig),
        out_shape=(jax.ShapeDtypeStruct(ar.shape, BF16), jax.ShapeDtypeStruct(ar.shape, BF16)),
        compiler_params=_params("parallel", "parallel"),
        name="dft_inner",
    )(ar, ai, twc, tws, kf, m_fwd, m_inv)


def _dft_outer_inv_kernel(br_ref, bi_ref, m_ref, u_ref, x0_ref, d_ref, o_ref):
    b = jnp.concatenate([br_ref[...].astype(BF16), bi_ref[...].astype(BF16)], axis=0)
    y = _dot(m_ref[...], b)
    u = u_ref[...].astype(F32)
    o_ref[...] = (x0_ref[...].astype(F32) * (y + u * d_ref[...])).astype(o_ref.dtype)


def _dft_outer_inv(br, bi, m_inv_outer, u2, x02, d_lanes):
    B, n1, W = br.shape
    lb = d_lanes.shape[1]
    blk = lambda rows: pl.BlockSpec((None, rows, lb), lambda b, j: (b, 0, j))
    return pl.pallas_call(
        _dft_outer_inv_kernel,
        grid=(B, W // lb),
        in_specs=[blk(n1), blk(n1), _full(m_inv_outer), blk(n1 // 2), blk(n1 // 2), _full(d_lanes)],
        out_specs=blk(n1 // 2),
        out_shape=jax.ShapeDtypeStruct((B, n1 // 2, W), BF16),
        compiler_params=_params("parallel", "parallel"),
        name="dft_outer_inv_gate",
    )(br, bi, m_inv_outer, u2, x02, d_lanes)


@functools.lru_cache(maxsize=None)
def _dft_tables(seq):
    n = 2 * seq
    n2 = DFT_N2
    n1 = n // n2
    k1 = np.arange(n1)[:, None]
    th = 2.0 * np.pi * ((k1 * np.arange(n1)[None, :]) % n1) / n1
    c1, s1 = np.cos(th), np.sin(th)
    m_outer_full = np.concatenate([c1, -s1], axis=0)
    m_outer_half = m_outer_full[:, :n1 // 2]
    m_outer_inv = np.concatenate([c1[:n1 // 2], -s1[:n1 // 2]], axis=1)
    ps = 2.0 * np.pi * ((np.arange(n2)[:, None] * np.arange(n2)[None, :]) % n2) / n2
    c2, s2 = np.cos(ps), np.sin(ps)
    m_fwd = np.block([[c2, s2], [-s2, c2]])
    m_inv = np.block([[c2, -s2], [s2, c2]])
    ph = 2.0 * np.pi * (k1 * np.arange(n2)[None, :]) / n
    f = lambda a: np.asarray(a, np.float32)
    return dict(n1=n1, n2=n2, m_outer_full=f(m_outer_full), m_outer_half=f(m_outer_half),
                m_outer_inv=f(m_outer_inv), m_fwd=f(m_fwd), m_inv=f(m_inv),
                twc=f(np.cos(ph))[:, :, None], tws=f(np.sin(ph))[:, :, None])


def _filter_features(seq):
    t = jnp.linspace(0.0, 1.0, seq, dtype=F32)[:, None]
    w = 2.0 * math.pi * jnp.arange(seq, dtype=F32) / seq
    f = jnp.linspace(1e-4, FILTER_BANDS - 1, FILTER_BANDS, dtype=F32)
    ang = w[:, None] * f[None, :]
    z = jnp.concatenate([t, jnp.cos(ang), -jnp.sin(ang)], axis=-1)
    z = jnp.concatenate([z, z[seq - 1:], z[:0:-1]], axis=0)
    return jnp.pad(z, ((0, 0), (0, LANES - FILTER_EMB)))


def _hyena(phy, lp):
    B, L, W = phy.shape
    c = W // 3
    tb = _dft_tables(L)
    n1, n2 = tb["n1"], tb["n2"]
    bf = lambda name: jnp.asarray(tb[name]).astype(BF16)
    twc, tws = jnp.asarray(tb["twc"]), jnp.asarray(tb["tws"])

    u, x0 = _hygate(phy, lp["conv_w"], lp["conv_b"])

    kern = _filters(_filter_features(L), lp["filt_w1"], lp["filt_b1"], lp["filt_f1"], lp["filt_w2"],
                    lp["filt_b2"], lp["filt_f2"], lp["filt_w3"], lp["deltas"])
    far, fai = _dft_outer(kern.reshape(1, n1, n2 * c), bf("m_outer_full"), F32)
    kf = _dft_spectrum(far.reshape(n1, n2, c), fai.reshape(n1, n2, c), twc, tws, bf("m_fwd"),
                       scale=1.0 / (2 * L))

    u2 = u.reshape(B, n1 // 2, n2 * c)
    ar, ai = _dft_outer(u2, bf("m_outer_half"), BF16)
    br, bi = _dft_inner(ar.reshape(B, n1, n2, c), ai.reshape(B, n1, n2, c), twc, tws, kf,
                        bf("m_fwd"), bf("m_inv"))
    lb = min(DFT_LANE_BLOCK, n2 * c)
    d_lanes = jnp.tile(lp["hyena_d"], (1, lb // c))
    hyo = _dft_outer_inv(br.reshape(B, n1, n2 * c), bi.reshape(B, n1, n2 * c), bf("m_outer_inv"),
                         u2, x0.reshape(B, n1 // 2, n2 * c), d_lanes)
    return hyo.reshape(B, L, c)


def _outproj_kernel(att_ref, hyo_ref, x_ref, wa_ref, wh_ref, bo_ref, g_ref, b_ref, wr_hi_ref, wr_lo_ref,
                    br_ref, x1_ref, comb_ref, *, alpha):
    m = _dot(att_ref[...], wa_ref[...]) + _dot(hyo_ref[...], wh_ref[...]) + bo_ref[...]
    x1 = _layer_norm(alpha * x_ref[...] + m, g_ref[...], b_ref[...])
    x1_ref[...] = x1

    xh, xl = _split_bf16(x1)
    lg = _dot(xh, wr_hi_ref[...]) + (_dot(xh, wr_lo_ref[...]) + _dot(xl, wr_hi_ref[...])) + br_ref[...]
    lane = lax.broadcasted_iota(jnp.int32, lg.shape, 1)
    big = jnp.int32(ROUTER_LANES)
    ninf = jnp.float32(-jnp.inf)
    first = lambda mask: jnp.min(jnp.where(mask, lane, big), axis=-1, keepdims=True)

    gmask = lane < N_GROUPS
    gmax = jnp.max(jnp.where(gmask, lg, ninf), axis=-1, keepdims=True)
    gsum = jnp.sum(jnp.where(gmask, jnp.exp(lg - gmax), 0.0), axis=-1, keepdims=True)
    g_w = 1.0 / gsum
    g_idx = first(gmask & (lg == gmax))
    lo = N_GROUPS + EXPERTS_PER_GROUP * g_idx
    emask = (lane >= lo) & (lane < lo + EXPERTS_PER_GROUP)
    e1 = jnp.max(jnp.where(emask, lg, ninf), axis=-1, keepdims=True)
    i1 = first(emask & (lg == e1))
    rest = emask & (lane != i1)
    e2 = jnp.max(jnp.where(rest, lg, ninf), axis=-1, keepdims=True)
    i2 = first(rest & (lg == e2))
    tt = jnp.exp(e2 - e1)
    w1 = g_w / (1.0 + tt)
    comb_ref[...] = (jnp.where(lane == i1, w1, 0.0) + jnp.where(lane == i2, w1 * tt, 0.0)
                     + jnp.where(lane == 0, g_idx.astype(F32), 0.0))


def _outproj_ln_router(att, hyo, x, wa, wh, bo, g, b, wr_hi, wr_lo, br, *, alpha):
    B, L, D = x.shape
    tm = PROJ_TILE
    tok = lambda w: pl.BlockSpec((None, tm, w), lambda bb, i: (bb, i, 0))
    return pl.pallas_call(
        functools.partial(_outproj_kernel, alpha=alpha),
        grid=(B, L // tm),
        in_specs=[tok(att.shape[2]), tok(hyo.shape[2]), tok(D)]
        + [_full(a) for a in (wa, wh, bo, g, b, wr_hi, wr_lo, br)],
        out_specs=(tok(D), tok(ROUTER_LANES)),
        out_shape=(jax.ShapeDtypeStruct((B, L, D), F32), jax.ShapeDtypeStruct((B, L, ROUTER_LANES), F32)),
        compiler_params=_params("parallel", "parallel"),
        name="outproj_ln_router",
    )(att, hyo, x, wa, wh, bo, g, b, wr_hi, wr_lo, br)


def _moe_kernel(meta_ref, x_ref, comb_ref, w1_ref, w3_ref, w2_ref, g_ref, b_ref, o_ref,
                xs_ref, ys_ref, cs_ref, pt_ref, *, alpha, tb, slots, tile):
    bi = pl.program_id(0)
    blk = pl.program_id(1)
    step = pl.program_id(2)
    grp = step // (EXPERTS_PER_GROUP // MOE_EXPERTS_PER_STEP)

    @pl.when(step == 0)
    def _():
        comb = comb_ref[...]
        starts = [meta_ref[bi, blk, g].astype(F32) for g in range(N_GROUPS)]
        gid_row = comb.T[0:1, :]
        grow = lax.broadcasted_iota(jnp.int32, (8, tb), 0).astype(F32)
        oh_row = jnp.where(gid_row == grow, 1.0, 0.0)
        r_i = lax.broadcasted_iota(jnp.int32, (tb, tb), 0)
        c_i = lax.broadcasted_iota(jnp.int32, (tb, tb), 1)
        earlier_row = jnp.where(r_i < c_i, 1.0, 0.0).astype(BF16)
        rank_row = jnp.sum(oh_row * _dot(oh_row.astype(BF16), earlier_row), axis=0, keepdims=True)
        pos_row = rank_row
        for g in range(N_GROUPS):
            pos_row = pos_row + oh_row[g:g + 1] * starts[g]
        slot_r = lax.broadcasted_iota(jnp.int32, (slots, tb), 0).astype(F32)
        p = jnp.where(slot_r == pos_row, 1.0, 0.0).astype(BF16)
        ch, cl = _split_bf16(comb)
        d = x_ref.shape[1]
        moved = _dot(p, jnp.concatenate([x_ref[...].astype(BF16), ch, cl], axis=1))
        xs_ref[...] = moved[:, :d].astype(BF16)
        cs_ref[...] = moved[:, d:d + ROUTER_LANES] + moved[:, d + ROUTER_LANES:]

        gid_col = comb[:, 0:1]
        glane = lax.broadcasted_iota(jnp.int32, comb.shape, 1).astype(F32)
        oh_col = jnp.where(gid_col == glane, 1.0, 0.0)
        earlier_col = jnp.where(c_i < r_i, 1.0, 0.0).astype(BF16)
        rank_col = jnp.sum(oh_col * _dot(earlier_col, oh_col.astype(BF16)), axis=1, keepdims=True)
        pos_col = rank_col
        for g in range(N_GROUPS):
            pos_col = pos_col + oh_col[:, g:g + 1] * starts[g]
        slot_c = lax.broadcasted_iota(jnp.int32, (tb, slots), 1).astype(F32)
        pt_ref[...] = jnp.where(slot_c == pos_col, 1.0, 0.0).astype(BF16)
        ys_ref[...] = jnp.zeros_like(ys_ref)

    start = meta_ref[bi, blk, grp]
    n_tiles = meta_ref[bi, blk, N_GROUPS + grp]

    def row_tile(i, carry):
        rows = pl.ds(pl.multiple_of(start + i * tile, MOE_GROUP_ALIGN), tile)
        xt = xs_ref[rows, :]
        c = cs_ref[rows, :]
        lane = lax.broadcasted_iota(jnp.int32, c.shape, 1)
        hid = [jax.nn.silu(_dot(xt, w1_ref[k])) * _dot(xt, w3_ref[k]) for k in range(MOE_EXPERTS_PER_STEP)]
        y = None
        for k in range(MOE_EXPERTS_PER_STEP):
            e = N_GROUPS + step * MOE_EXPERTS_PER_STEP + k
            ce = jnp.sum(jnp.where(lane == e, c, 0.0), axis=-1, keepdims=True)
            yk = ce * _dot(hid[k].astype(BF16), w2_ref[k])
            y = yk if y is None else y + yk
        ys_ref[rows, :] += y
        return carry

    lax.fori_loop(0, n_tiles, row_tile, 0)

    @pl.when(step == pl.num_programs(2) - 1)
    def _():
        f = _dot(pt_ref[...], ys_ref[...].astype(BF16))
        o_ref[...] = _layer_norm(alpha * x_ref[...] + f, g_ref[...], b_ref[...])


def _moe_ln(x1, comb, w1, w3, w2, g, b, *, alpha):
    B, L, D = x1.shape
    E, _, de = w1.shape
    tb = min(MOE_BLOCK, L)
    tile = MOE_ROW_TILE
    per = MOE_EXPERTS_PER_STEP
    nblk = L // tb
    slots = tb + 2 * tile
    gid = comb[..., 0].astype(jnp.int32).reshape(B, nblk, tb)
    cnt = jnp.sum(gid[..., None] == jnp.arange(N_GROUPS), axis=2).astype(jnp.int32)
    padded = (cnt + MOE_GROUP_ALIGN - 1) // MOE_GROUP_ALIGN * MOE_GROUP_ALIGN
    meta = jnp.concatenate([jnp.cumsum(padded, axis=-1) - padded, (cnt + tile - 1) // tile], axis=-1)

    tok = lambda w: pl.BlockSpec((None, tb, w), lambda bb, i, e, m: (bb, i, 0))
    const = lambda a: pl.BlockSpec(a.shape, lambda bb, i, e, m: (0,) * a.ndim)
    return pl.pallas_call(
        functools.partial(_moe_kernel, alpha=alpha, tb=tb, slots=slots, tile=tile),
        grid_spec=pltpu.PrefetchScalarGridSpec(
            num_scalar_prefetch=1,
            grid=(B, nblk, E // per),
            in_specs=[tok(D), tok(ROUTER_LANES),
                      pl.BlockSpec((per, D, de), lambda bb, i, s, m: (s, 0, 0)),
                      pl.BlockSpec((per, D, de), lambda bb, i, s, m: (s, 0, 0)),
                      pl.BlockSpec((per, de, D), lambda bb, i, s, m: (s, 0, 0)),
                      const(g), const(b)],
            out_specs=tok(D),
            scratch_shapes=[pltpu.VMEM((slots, D), BF16), pltpu.VMEM((slots, D), F32),
                            pltpu.VMEM((slots, ROUTER_LANES), F32), pltpu.VMEM((tb, slots), BF16)]),
        out_shape=jax.ShapeDtypeStruct((B, L, D), F32),
        compiler_params=_params("parallel", "parallel", "arbitrary"),
        name="moe_ln",
    )(meta, x1, comb, w1, w3, w2, g, b)


def _alibi_tables():
    t = ATT_TILE
    slopes = jnp.asarray(LOG2E * 2.0 ** (-8.0 * np.arange(1, ATT_HEADS + 1) / ATT_HEADS), dtype=F32)
    off = (jnp.arange(t)[:, None] - jnp.arange(t)[None, :]).astype(F32)
    dpos = slopes[:, None, None] * off[None]
    return slopes, jnp.stack([dpos, -jnp.abs(dpos), -dpos], axis=1)


def _prep_layer(l, depth, w_in, b_in, conv_w, conv_b, lam_q1, lam_k1, lam_q2, lam_k2, subln_g,
                filt_w1, filt_b1, filt_freq1, filt_w2, filt_b2, filt_freq2, filt_w3, hyena_d,
                w_out, b_out, ln1_g, ln1_b, router_group_w, router_group_b, router_expert_w,
                router_expert_b, exp_w1, exp_w3, exp_w2, ln2_g, ln2_b):
    a0, a1, a2 = ATT_QK_WIDTH, 2 * ATT_QK_WIDTH, 2 * ATT_QK_WIDTH + ATT_WIDTH
    scale = ATT_HEAD_DIM ** -0.5 * LOG2E
    row = lambda v: v.reshape(1, -1)
    w, b = w_in[l], b_in[l]
    c = hyena_d.shape[1]
    wr = jnp.concatenate([router_group_w[l], router_expert_w[l]], axis=1)
    wr = jnp.pad(wr, ((0, 0), (0, ROUTER_LANES - wr.shape[1])))
    wr_hi = wr.astype(BF16)
    br = jnp.concatenate([router_group_b[l], router_expert_b[l]])
    max_decay = abs(math.log(DECAY_TARGET) / FAST_DECAY_PCT)
    min_decay = abs(math.log(DECAY_TARGET) / SLOW_DECAY_PCT)
    return dict(
        lam_init=0.8 - 0.6 * math.exp(-0.3 * l),
        alpha=(2.0 * depth) ** 0.25,
        wq=(w[:, :a0] * scale).astype(BF16), bq=row(b[:a0] * scale),
        wk=w[:, a0:a1].astype(BF16), bk=row(b[a0:a1]),
        wvt=w[:, a1:a2].T.astype(BF16), bvt=b[a1:a2].reshape(-1, 1),
        why=w[:, a2:].astype(BF16), bhy=row(b[a2:]),
        conv_w=conv_w[l], conv_b=row(conv_b[l]),
        lamv=jnp.stack([lam_q1[l], lam_k1[l], lam_q2[l], lam_k2[l]]).astype(F32),
        subln_g=row(subln_g[l]),
        filt_w1=jnp.pad(filt_w1[l], ((0, LANES - FILTER_EMB), (0, 0))), filt_b1=row(filt_b1[l]),
        filt_f1=row(filt_freq1[l]), filt_w2=filt_w2[l], filt_b2=row(filt_b2[l]), filt_f2=row(filt_freq2[l]),
        filt_w3=filt_w3[l],
        deltas=row(jnp.linspace(min_decay, max_decay, c, dtype=F32)),
        hyena_d=row(hyena_d[l]),
        wa=w_out[l][:ATT_WIDTH].astype(BF16), wh=w_out[l][ATT_WIDTH:].astype(BF16), bo=row(b_out[l]),
        ln1_g=row(ln1_g[l]), ln1_b=row(ln1_b[l]),
        wr_hi=wr_hi, wr_lo=(wr - wr_hi.astype(F32)).astype(BF16),
        br=row(jnp.pad(br, (0, ROUTER_LANES - br.shape[0]))),
        w1=exp_w1[l].astype(BF16), w3=exp_w3[l].astype(BF16), w2=exp_w2[l].astype(BF16),
        ln2_g=row(ln2_g[l]), ln2_b=row(ln2_b[l]),
    )


def _layer(x, lp, alibi):
    slopes, dist = alibi
    q, k, vt, phy = _inproj(x, lp["wq"], lp["bq"], lp["wk"], lp["bk"], lp["wvt"], lp["bvt"],
                            lp["why"], lp["bhy"])
    att = _attention(q, k, vt, slopes, dist, lp["lamv"], lp["subln_g"],
                     lam_init=lp["lam_init"])
    hyo = _hyena(phy, lp)
    x1, comb = _outproj_ln_router(att, hyo, x, lp["wa"], lp["wh"], lp["bo"], lp["ln1_g"], lp["ln1_b"],
                                  lp["wr_hi"], lp["wr_lo"], lp["br"], alpha=lp["alpha"])
    return _moe_ln(x1, comb, lp["w1"], lp["w3"], lp["w2"], lp["ln2_g"], lp["ln2_b"], alpha=lp["alpha"])


def kernel(x_prompt, x_sample, w_in, b_in, conv_w, conv_b, lam_q1, lam_k1, lam_q2, lam_k2, subln_g, filt_w1, filt_b1, filt_freq1, filt_w2, filt_b2, filt_freq2, filt_w3, hyena_d, w_out, b_out, ln1_g, ln1_b, router_group_w, router_group_b, router_expert_w, router_expert_b, exp_w1, exp_w3, exp_w2, ln2_g, ln2_b):
    params = (w_in, b_in, conv_w, conv_b, lam_q1, lam_k1, lam_q2, lam_k2, subln_g, filt_w1, filt_b1,
              filt_freq1, filt_w2, filt_b2, filt_freq2, filt_w3, hyena_d, w_out, b_out, ln1_g, ln1_b,
              router_group_w, router_group_b, router_expert_w, router_expert_b, exp_w1, exp_w3, exp_w2,
              ln2_g, ln2_b)
    depth = w_in.shape[0]
    layers = [_prep_layer(l, depth, *params) for l in range(depth)]
    alibi = _alibi_tables()

    def trunk(x):
        for lp in layers:
            x = _layer(x, lp, alibi)
        return x

    return (trunk(x_prompt), trunk(x_sample))
```

```python
import functools
import math

import numpy as np
import jax
import jax.numpy as jnp
from jax import lax
from jax.experimental import pallas as pl
from jax.experimental.pallas import tpu as pltpu

F32 = jnp.float32
BF16 = jnp.bfloat16

ATT_HEADS = 4
ATT_HEAD_DIM = 64
ATT_V_DIM = 2 * ATT_HEAD_DIM
ATT_QK_WIDTH = ATT_HEADS * 2 * ATT_HEAD_DIM
ATT_WIDTH = ATT_HEADS * ATT_V_DIM
FILTER_EMB = 33
FILTER_BANDS = (FILTER_EMB - 1) // 2
DECAY_TARGET = 1e-2
FAST_DECAY_PCT = 0.3
SLOW_DECAY_PCT = 1.5
FILTER_SHIFT = 0.05
N_GROUPS = 4
EXPERTS_PER_GROUP = 4
N_EXPERTS = N_GROUPS * EXPERTS_PER_GROUP
LN_EPS = 1e-5
RMS_EPS = 1e-5

LANES = 128
BF16_SUBLANES = 16
VMEM_LIMIT_BYTES = 56 * 1024 * 1024

ATT_TILE = 256
ATT_Q_TILES = 1
ATT_KEY_BLOCK = 1024
ATT_LOOP_BLOCKS = 4
VT_ROWS = ATT_V_DIM + BF16_SUBLANES
PROJ_TILE = 1024
GATE_TILE = 1024
FILTER_TILE = 512
DFT_N2 = 128
DFT_LANE_BLOCK = 4096
DFT_K1_BLOCK = 16
MOE_BLOCK = 1024
MOE_ROW_TILE = 128
MOE_GROUP_ALIGN = BF16_SUBLANES
MOE_EXPERTS_PER_STEP = 2
ROUTER_LANES = LANES
NEG_BIG = -1e30
LOG2E = math.log2(math.e)


def _params(*sem):
    return pltpu.CompilerParams(dimension_semantics=sem, vmem_limit_bytes=VMEM_LIMIT_BYTES)


def _full(a):
    nd = a.ndim
    return pl.BlockSpec(a.shape, lambda *_: (0,) * nd)


def _dot(a, b):
    return jnp.dot(a, b, preferred_element_type=F32)


def _dot_nt(a, b):
    return lax.dot_general(a, b, (((1,), (1,)), ((), ())), preferred_element_type=F32)


def _split_bf16(x):
    hi = x.astype(BF16)
    lo = (x - hi.astype(F32)).astype(BF16)
    return hi, lo


def _dot_3pass(a, b):
    ah, al = _split_bf16(a)
    bh, bl = _split_bf16(b)
    return _dot(ah, bh) + (_dot(ah, bl) + _dot(al, bh))


def _layer_norm(r, g, b):
    mu = jnp.mean(r, axis=-1, keepdims=True)
    c = r - mu
    var = jnp.mean(c * c, axis=-1, keepdims=True)
    return c * lax.rsqrt(var + LN_EPS) * g + b


def _inproj_kernel(x_ref, wq_ref, bq_ref, wk_ref, bk_ref, wvt_ref, bvt_ref, why_ref, bhy_ref,
                   q_ref, k_ref, vt_ref, phy_ref):
    xb = x_ref[...].astype(BF16)
    tm = xb.shape[0]
    q_ref[...] = (_dot(xb, wq_ref[...]) + bq_ref[...]).astype(BF16)
    k_ref[...] = (_dot(xb, wk_ref[...]) + bk_ref[...]).astype(BF16)
    vt = (_dot_nt(wvt_ref[...], xb) + bvt_ref[...]).astype(BF16)
    row = lax.broadcasted_iota(jnp.int32, (BF16_SUBLANES, tm), 0)
    ones_rows = jnp.where(row == 0, 1.0, 0.0).astype(BF16)
    for h in range(ATT_HEADS):
        vt_ref[h, :ATT_V_DIM, :] = vt[h * ATT_V_DIM:(h + 1) * ATT_V_DIM]
        vt_ref[h, ATT_V_DIM:, :] = ones_rows
    phy_ref[...] = (_dot(xb, why_ref[...]) + bhy_ref[...]).astype(phy_ref.dtype)


def _inproj(x, wq, bq, wk, bk, wvt, bvt, why, bhy):
    B, L, D = x.shape
    tm, kb = PROJ_TILE, min(ATT_KEY_BLOCK, L)
    per = kb // tm
    hyw = why.shape[1]
    tok = lambda w: pl.BlockSpec((None, tm, w), lambda b, i: (b, i, 0))
    return pl.pallas_call(
        _inproj_kernel,
        grid=(B, L // tm),
        in_specs=[tok(D)] + [_full(a) for a in (wq, bq, wk, bk, wvt, bvt, why, bhy)],
        out_specs=(tok(ATT_QK_WIDTH), tok(ATT_QK_WIDTH),
                   pl.BlockSpec((None, None, ATT_HEADS, VT_ROWS, tm),
                                lambda b, i: (b, i // per, 0, 0, i % per)),
                   tok(hyw)),
        out_shape=(jax.ShapeDtypeStruct((B, L, ATT_QK_WIDTH), BF16),
                   jax.ShapeDtypeStruct((B, L, ATT_QK_WIDTH), BF16),
                   jax.ShapeDtypeStruct((B, L // kb, ATT_HEADS, VT_ROWS, kb), BF16),
                   jax.ShapeDtypeStruct((B, L, hyw), BF16)),
        compiler_params=_params("parallel", "parallel"),
        name="inproj",
    )(x, wq, bq, wk, bk, wvt, bvt, why, bhy)


def _attn_kernel(slopes_ref, q_ref, k_ref, vt_ref, dist_ref, lam_ref, g_ref,
                 o_ref, acc_ref, s_ref, *, t, nq, kb, nkb, lam_init):
    h = pl.program_id(1)
    q0 = pl.program_id(2) * nq
    ns = kb // t
    slope = slopes_ref[h]
    q = q_ref[...]
    lane = lax.broadcasted_iota(jnp.int32, q.shape, 1)
    zero = jnp.zeros_like(q)
    qmaps = (jnp.where(lane < ATT_HEAD_DIM, q, zero), jnp.where(lane >= ATT_HEAD_DIM, q, zero))
    acc_ref[...] = jnp.zeros_like(acc_ref)

    def produce(qt, j, slot):
        kblk = k_ref[pl.ds(pl.multiple_of(j * kb, kb), kb), :]
        for mi in range(2):
            s = _dot_nt(kblk, qmaps[mi][qt * t:(qt + 1) * t])
            for st in range(ns):
                sel = jnp.clip(j * ns + st - (q0 + qt), -1, 1) + 1
                s_ref[qt, slot, mi, st * t:(st + 1) * t, :] = s[st * t:(st + 1) * t] + dist_ref[sel]

    def consume(qt, j, slot, ms):
        vt = vt_ref[j]
        cs = [-slope * (jnp.abs(j * ns + st - (q0 + qt)) * t).astype(F32) for st in range(ns)]
        out = []
        for mi in range(2):
            sub = [s_ref[qt, slot, mi, st * t:(st + 1) * t, :] for st in range(ns)]
            m_new = ms[mi]
            for st in range(ns):
                m_new = jnp.maximum(m_new, jnp.max(sub[st], axis=0, keepdims=True) + cs[st])
            e = jnp.concatenate([jnp.exp2(sub[st] - (m_new - cs[st])).astype(BF16) for st in range(ns)],
                                axis=0)
            acc_ref[qt, mi] = acc_ref[qt, mi] * jnp.exp2(ms[mi] - m_new) + _dot(vt, e)
            out.append(m_new)
        return tuple(out)

    def advance(j_next, slot_next, j, slot, ms):
        if j_next is not None:
            for qt in range(nq):
                produce(qt, j_next, slot_next)
        return tuple(consume(qt, j, slot, ms[qt]) for qt in range(nq))

    m0 = jnp.full((1, t), NEG_BIG, F32)
    ms = ((m0, m0),) * nq
    for qt in range(nq):
        produce(qt, 0, 0)
    def run(first, count, last, ms):
        for b in range(count):
            j = first + b
            nxt = None if (last and b == count - 1) else j + 1
            ms = advance(nxt, None if nxt is None else (b + 1) % 2, j, b % 2, ms)
        return ms

    per = ATT_LOOP_BLOCKS if nkb >= 2 * ATT_LOOP_BLOCKS else min(2, nkb)
    assert per % 2 == 0 or nkb == 1
    assert nkb % per == 0
    trips = nkb // per - 1
    if trips > 0:
        trips = jnp.minimum(pl.program_id(0) + trips, trips)
        ms = lax.fori_loop(0, trips, lambda i, ms: run(i * per, per, False, ms), ms)
    run(nkb - per, per, True, ms)

    lv = lam_ref[...]
    lam = (jnp.exp(jnp.sum(lv[0:1] * lv[1:2], axis=-1, keepdims=True))
           - jnp.exp(jnp.sum(lv[2:3] * lv[3:4], axis=-1, keepdims=True)) + lam_init)
    for qt in range(nq):
        a0 = acc_ref[qt, 0]
        a1 = acc_ref[qt, 1]
        o0 = a0[:ATT_V_DIM] / a0[ATT_V_DIM:ATT_V_DIM + 1]
        o1 = a1[:ATT_V_DIM] / a1[ATT_V_DIM:ATT_V_DIM + 1]
        d = o0 - lam * o1
        d = d * lax.rsqrt(jnp.mean(d * d, axis=0, keepdims=True) + RMS_EPS)
        o_ref[qt * t:(qt + 1) * t, :] = (d.T * (g_ref[...] * (1.0 - lam_init))).astype(o_ref.dtype)


def _attention(q, k, vt, slopes, dist, lamv, g, *, lam_init):
    B, L, _ = q.shape
    t = ATT_TILE
    nq = min(ATT_Q_TILES, L // t)
    nkb, kb = vt.shape[1], vt.shape[4]
    assert nkb == 1 or nkb % 2 == 0
    return pl.pallas_call(
        functools.partial(_attn_kernel, t=t, nq=nq, kb=kb, nkb=nkb, lam_init=lam_init),
        grid=(B, ATT_HEADS, L // (nq * t)),
        in_specs=[pl.BlockSpec(memory_space=pltpu.SMEM),
                  pl.BlockSpec((None, nq * t, LANES), lambda b, h, i: (b, i, h)),
                  pl.BlockSpec((None, L, LANES), lambda b, h, i: (b, 0, h)),
                  pl.BlockSpec((None, nkb, None, VT_ROWS, kb), lambda b, h, i: (b, 0, h, 0, 0)),
                  pl.BlockSpec((None, 3, t, t), lambda b, h, i: (h, 0, 0, 0)),
                  _full(lamv), _full(g)],
        out_specs=pl.BlockSpec((None, nq * t, LANES), lambda b, h, i: (b, i, h)),
        out_shape=jax.ShapeDtypeStruct((B, L, ATT_WIDTH), BF16),
        scratch_shapes=[pltpu.VMEM((nq, 2, VT_ROWS, t), F32), pltpu.VMEM((nq, 2, 2, kb, t), F32)],
        compiler_params=_params("parallel", "parallel", "arbitrary"),
        name="diff_attention",
    )(slopes, q, k, vt, dist, lamv, g)


def _hygate_kernel(p_ref, prev_ref, next_ref, w_ref, b_ref, u_ref, x0_ref, *, tl, c):
    i = pl.program_id(1)
    last = pl.num_programs(1) - 1
    x = p_ref[...].astype(F32)
    sub = prev_ref.shape[0]
    prev_row = jnp.where(i == 0, 0.0, prev_ref[...].astype(F32)[sub - 1:sub, :])
    next_row = jnp.where(i == last, 0.0, next_ref[...].astype(F32)[0:1, :])
    row = lax.broadcasted_iota(jnp.int32, x.shape, 0)
    xm = jnp.where(row == 0, prev_row, pltpu.roll(x, 1, 0))
    xp = jnp.where(row == tl - 1, next_row, pltpu.roll(x, tl - 1, 0))
    w = w_ref[...]
    hy = xm * w[0:1] + x * w[1:2] + xp * w[2:3] + b_ref[...]
    x0_ref[...] = hy[:, :c].astype(x0_ref.dtype)
    u_ref[...] = (hy[:, 2 * c:] * hy[:, c:2 * c]).astype(u_ref.dtype)


def _hygate(phy, conv_w, conv_b):
    B, L, W = phy.shape
    c = W // 3
    tl = GATE_TILE
    sub = BF16_SUBLANES
    nb = tl // sub
    last_blk = L // sub - 1
    return pl.pallas_call(
        functools.partial(_hygate_kernel, tl=tl, c=c),
        grid=(B, L // tl),
        in_specs=[pl.BlockSpec((None, tl, W), lambda b, i: (b, i, 0)),
                  pl.BlockSpec((None, sub, W), lambda b, i: (b, jnp.maximum(i * nb - 1, 0), 0)),
                  pl.BlockSpec((None, sub, W), lambda b, i: (b, jnp.minimum((i + 1) * nb, last_blk), 0)),
                  _full(conv_w), _full(conv_b)],
        out_specs=(pl.BlockSpec((None, tl, c), lambda b, i: (b, i, 0)),
                   pl.BlockSpec((None, tl, c), lambda b, i: (b, i, 0))),
        out_shape=(jax.ShapeDtypeStruct((B, L, c), BF16), jax.ShapeDtypeStruct((B, L, c), BF16)),
        compiler_params=_params("parallel", "parallel"),
        name="hyena_gate",
    )(phy, phy, phy, conv_w, conv_b)


def _filter_kernel(z_ref, w1_ref, b1_ref, f1_ref, w2_ref, b2_ref, f2_ref, w3_ref, deltas_ref,
                   k_ref, *, tl, seq, c):
    i = pl.program_id(0)
    h = jnp.sin(f1_ref[...] * (_dot_3pass(z_ref[...], w1_ref[...]) + b1_ref[...]))
    h = jnp.sin(f2_ref[...] * (_dot_3pass(h, w2_ref[...]) + b2_ref[...]))
    h = _dot_3pass(h, w3_ref[...])
    n = lax.broadcasted_iota(jnp.int32, (tl, c), 0) + i * tl
    lag = jnp.where(n < seq, n, 2 * seq - n).astype(F32)
    decay = jnp.exp(-(lag * (1.0 / (seq - 1))) * deltas_ref[...]) + FILTER_SHIFT
    k_ref[...] = jnp.where(n == seq, 0.0, h * decay)


def _filters(z2, w1, b1, f1, w2, b2, f2, w3, deltas):
    n = z2.shape[0]
    L = n // 2
    c = w3.shape[1] // 2
    tl = FILTER_TILE
    half = L // tl
    return pl.pallas_call(
        functools.partial(_filter_kernel, tl=tl, seq=L, c=c),
        grid=(n // tl,),
        in_specs=[pl.BlockSpec((tl, z2.shape[1]), lambda i: (i, 0))]
        + [_full(a) for a in (w1, b1, f1, w2, b2, f2)]
        + [pl.BlockSpec((w3.shape[0], c), lambda i: (0, i // half)), _full(deltas)],
        out_specs=pl.BlockSpec((tl, c), lambda i: (i, 0)),
        out_shape=jax.ShapeDtypeStruct((n, c), F32),
        compiler_params=_params("parallel"),
        name="hyena_filter",
    )(z2, w1, b1, f1, w2, b2, f2, w3, deltas)


def _dft_outer_kernel(u_ref, m_ref, ar_ref, ai_ref, *, n1):
    a = _dot(m_ref[...], u_ref[...].astype(BF16))
    ar_ref[...] = a[:n1].astype(ar_ref.dtype)
    ai_ref[...] = a[n1:].astype(ai_ref.dtype)


def _dft_outer(u2, m_outer, out_dtype):
    B, n1_in, W = u2.shape
    n1 = m_outer.shape[0] // 2
    lb = min(DFT_LANE_BLOCK, W)
    blk = lambda rows: pl.BlockSpec((None, rows, lb), lambda b, j: (b, 0, j))
    return pl.pallas_call(
        functools.partial(_dft_outer_kernel, n1=n1),
        grid=(B, W // lb),
        in_specs=[blk(n1_in), _full(m_outer)],
        out_specs=(blk(n1), blk(n1)),
        out_shape=(jax.ShapeDtypeStruct((B, n1, W), out_dtype), jax.ShapeDtypeStruct((B, n1, W), out_dtype)),
        compiler_params=_params("parallel", "parallel"),
        name="dft_outer",
    )(u2, m_outer)


def _dft_outer_tw_kernel(u_ref, m_ref, ar_ref, ai_ref, *, n1, c):
    ub = u_ref[...].astype(BF16)
    for j in range(m_ref.shape[0]):
        a = _dot(m_ref[j], ub[:, j * c:(j + 1) * c])
        ar_ref[:, j * c:(j + 1) * c] = a[:n1].astype(ar_ref.dtype)
        ai_ref[:, j * c:(j + 1) * c] = a[n1:].astype(ai_ref.dtype)


def _dft_outer_tw(u2, m_tw, c):
    B, n1_in, W = u2.shape
    n1 = m_tw.shape[1] // 2
    lb = min(DFT_LANE_BLOCK, W)
    nb = lb // c
    blk = lambda rows: pl.BlockSpec((None, rows, lb), lambda b, j: (b, 0, j))
    return pl.pallas_call(
        functools.partial(_dft_outer_tw_kernel, n1=n1, c=c),
        grid=(B, W // lb),
        in_specs=[blk(n1_in), pl.BlockSpec((nb, 2 * n1, n1_in), lambda b, j: (j, 0, 0))],
        out_specs=(blk(n1), blk(n1)),
        out_shape=(jax.ShapeDtypeStruct((B, n1, W), BF16), jax.ShapeDtypeStruct((B, n1, W), BF16)),
        compiler_params=_params("parallel", "parallel"),
        name="dft_outer_tw",
    )(u2, m_tw)


def _twiddle(ar, ai, cph, sph, conj):
    if conj:
        return ar * cph - ai * sph, ai * cph + ar * sph
    return ar * cph + ai * sph, ai * cph - ar * sph


def _dft_spectrum_kernel(ar_ref, ai_ref, twc_ref, tws_ref, mf_ref, kf_ref, *, kb, n2, scale):
    for r in range(kb):
        pr, pi = _twiddle(ar_ref[r], ai_ref[r], twc_ref[r], tws_ref[r], conj=False)
        a = jnp.concatenate([pr.astype(BF16), pi.astype(BF16)], axis=0)
        kf_ref[r] = _dot(mf_ref[...], a) * scale


def _dft_spectrum(ar, ai, twc, tws, m_fwd, *, scale):
    n1, n2, c = ar.shape
    kb = DFT_K1_BLOCK
    blk = lambda rows, w: pl.BlockSpec((kb, rows, w), lambda i: (i, 0, 0))
    return pl.pallas_call(
        functools.partial(_dft_spectrum_kernel, kb=kb, n2=n2, scale=scale),
        grid=(n1 // kb,),
        in_specs=[blk(n2, c), blk(n2, c), blk(n2, 1), blk(n2, 1), _full(m_fwd)],
        out_specs=blk(2 * n2, c),
        out_shape=jax.ShapeDtypeStruct((n1, 2 * n2, c), F32),
        compiler_params=_params("parallel"),
        name="dft_filter_spectrum",
    )(ar, ai, twc, tws, m_fwd)


def _dft_inner_kernel(ar_ref, ai_ref, kf_ref, mf_ref, mi_ref, br_ref, bi_ref, *, kb, n2):
    for r in range(kb):
        a = jnp.concatenate([ar_ref[r], ai_ref[r]], axis=0)
        x = _dot(mf_ref[...], a)
        xr, xi = x[:n2], x[n2:]
        kr, ki = kf_ref[r, :n2], kf_ref[r, n2:]
        y = jnp.concatenate([(xr * kr - xi * ki).astype(BF16), (xr * ki + xi * kr).astype(BF16)], axis=0)
        bb = _dot(mi_ref[...], y)
        br_ref[r] = bb[:n2].astype(br_ref.dtype)
        bi_ref[r] = bb[n2:].astype(bi_ref.dtype)


def _dft_inner(ar, ai, kf, m_fwd, m_inv):
    B, n1, n2, c = ar.shape
    kb = DFT_K1_BLOCK
    sig = pl.BlockSpec((None, kb, n2, c), lambda i, b: (b, i, 0, 0))
    return pl.pallas_call(
        functools.partial(_dft_inner_kernel, kb=kb, n2=n2),
        grid=(n1 // kb, B),
        in_specs=[sig, sig, pl.BlockSpec((kb, 2 * n2, c), lambda i, b: (i, 0, 0)), _full(m_fwd), _full(m_inv)],
        out_specs=(sig, sig),
        out_shape=(jax.ShapeDtypeStruct(ar.shape, BF16), jax.ShapeDtypeStruct(ar.shape, BF16)),
        compiler_params=_params("parallel", "parallel"),
        name="dft_inner",
    )(ar, ai, kf, m_fwd, m_inv)


def _dft_outer_inv_kernel(br_ref, bi_ref, m_ref, u_ref, x0_ref, d_ref, o_ref, *, c):
    b = jnp.concatenate([br_ref[...], bi_ref[...]], axis=0)
    u = u_ref[...].astype(F32)
    x0 = x0_ref[...].astype(F32)
    d = d_ref[...]
    for j in range(m_ref.shape[0]):
        cols = slice(j * c, (j + 1) * c)
        y = _dot(m_ref[j], b[:, cols])
        o_ref[:, cols] = (x0[:, cols] * (y + u[:, cols] * d[:, cols])).astype(o_ref.dtype)


def _dft_outer_inv(br, bi, m_tw_inv, u2, x02, d_lanes, c):
    B, n1, W = br.shape
    lb = d_lanes.shape[1]
    nb = lb // c
    blk = lambda rows: pl.BlockSpec((None, rows, lb), lambda b, j: (b, 0, j))
    return pl.pallas_call(
        functools.partial(_dft_outer_inv_kernel, c=c),
        grid=(B, W // lb),
        in_specs=[blk(n1), blk(n1), pl.BlockSpec((nb, n1 // 2, 2 * n1), lambda b, j: (j, 0, 0)),
                  blk(n1 // 2), blk(n1 // 2), _full(d_lanes)],
        out_specs=blk(n1 // 2),
        out_shape=jax.ShapeDtypeStruct((B, n1 // 2, W), BF16),
        compiler_params=_params("parallel", "parallel"),
        name="dft_outer_inv_gate",
    )(br, bi, m_tw_inv, u2, x02, d_lanes)


@functools.lru_cache(maxsize=None)
def _dft_tables(seq):
    n = 2 * seq
    n2 = DFT_N2
    n1 = n // n2
    k1 = np.arange(n1)[:, None]
    th = 2.0 * np.pi * ((k1 * np.arange(n1)[None, :]) % n1) / n1
    c1, s1 = np.cos(th), np.sin(th)
    m_outer_full = np.concatenate([c1, -s1], axis=0)
    m_outer_half = m_outer_full[:, :n1 // 2]
    m_outer_inv = np.concatenate([c1[:n1 // 2], -s1[:n1 // 2]], axis=1)
    ps = 2.0 * np.pi * ((np.arange(n2)[:, None] * np.arange(n2)[None, :]) % n2) / n2
    c2, s2 = np.cos(ps), np.sin(ps)
    m_fwd = np.block([[c2, s2], [-s2, c2]])
    m_inv = np.block([[c2, -s2], [s2, c2]])
    ph = 2.0 * np.pi * (k1 * np.arange(n2)[None, :]) / n
    f = lambda a: np.asarray(a, np.float32)
    return dict(n1=n1, n2=n2, m_outer_full=f(m_outer_full), m_outer_half=f(m_outer_half),
                m_outer_inv=f(m_outer_inv), m_fwd=f(m_fwd), m_inv=f(m_inv),
                twc=f(np.cos(ph))[:, :, None], tws=f(np.sin(ph))[:, :, None])


def _filter_features(seq):
    t = jnp.linspace(0.0, 1.0, seq, dtype=F32)[:, None]
    w = 2.0 * math.pi * jnp.arange(seq, dtype=F32) / seq
    f = jnp.linspace(1e-4, FILTER_BANDS - 1, FILTER_BANDS, dtype=F32)
    ang = w[:, None] * f[None, :]
    z = jnp.concatenate([t, jnp.cos(ang), -jnp.sin(ang)], axis=-1)
    z = jnp.concatenate([z, z[seq - 1:], z[:0:-1]], axis=0)
    return jnp.pad(z, ((0, 0), (0, LANES - FILTER_EMB)))


def _twiddled_outer_matrices(seq):
    n = 2 * seq
    n2 = DFT_N2
    n1 = n // n2
    k1 = jnp.arange(n1, dtype=jnp.int32)
    pos = n2 * jnp.arange(n1 // 2, dtype=jnp.int32)[None, :] + jnp.arange(n2, dtype=jnp.int32)[:, None]
    ang = ((k1[None, :, None] * pos[:, None, :]) % n).astype(F32) * (2.0 * math.pi / n)
    fwd = jnp.concatenate([jnp.cos(ang), -jnp.sin(ang)], axis=1)
    return fwd.astype(BF16), fwd.transpose(0, 2, 1).astype(BF16)


def _hyena(phy, lp):
    B, L, W = phy.shape
    c = W // 3
    tb = _dft_tables(L)
    n1, n2 = tb["n1"], tb["n2"]
    bf = lambda name: jnp.asarray(tb[name]).astype(BF16)
    twc, tws = jnp.asarray(tb["twc"]), jnp.asarray(tb["tws"])

    u, x0 = _hygate(phy, lp["conv_w"], lp["conv_b"])

    kern = _filters(_filter_features(L), lp["filt_w1"], lp["filt_b1"], lp["filt_f1"], lp["filt_w2"],
                    lp["filt_b2"], lp["filt_f2"], lp["filt_w3"], lp["deltas"])
    far, fai = _dft_outer(kern.reshape(1, n1, n2 * c), bf("m_outer_full"), F32)
    kf = _dft_spectrum(far.reshape(n1, n2, c), fai.reshape(n1, n2, c), twc, tws, bf("m_fwd"),
                       scale=1.0 / (2 * L))

    m_tw, m_tw_inv = _twiddled_outer_matrices(L)
    u2 = u.reshape(B, n1 // 2, n2 * c)
    ar, ai = _dft_outer_tw(u2, m_tw, c)
    br, bi = _dft_inner(ar.reshape(B, n1, n2, c), ai.reshape(B, n1, n2, c), kf, bf("m_fwd"), bf("m_inv"))
    lb = min(DFT_LANE_BLOCK, n2 * c)
    d_lanes = jnp.tile(lp["hyena_d"], (1, lb // c))
    hyo = _dft_outer_inv(br.reshape(B, n1, n2 * c), bi.reshape(B, n1, n2 * c), m_tw_inv,
                         u2, x0.reshape(B, n1 // 2, n2 * c), d_lanes, c)
    return hyo.reshape(B, L, c)


def _outproj_kernel(att_ref, hyo_ref, x_ref, wa_ref, wh_ref, bo_ref, g_ref, b_ref, wr_hi_ref, wr_lo_ref,
                    br_ref, x1_ref, comb_ref, *, alpha):
    m = _dot(att_ref[...], wa_ref[...]) + _dot(hyo_ref[...], wh_ref[...]) + bo_ref[...]
    x1 = _layer_norm(alpha * x_ref[...] + m, g_ref[...], b_ref[...])
    x1_ref[...] = x1

    xh, xl = _split_bf16(x1)
    lg = _dot(xh, wr_hi_ref[...]) + (_dot(xh, wr_lo_ref[...]) + _dot(xl, wr_hi_ref[...])) + br_ref[...]
    lane = lax.broadcasted_iota(jnp.int32, lg.shape, 1)
    big = jnp.int32(ROUTER_LANES)
    ninf = jnp.float32(-jnp.inf)
    first = lambda mask: jnp.min(jnp.where(mask, lane, big), axis=-1, keepdims=True)

    gmask = lane < N_GROUPS
    gmax = jnp.max(jnp.where(gmask, lg, ninf), axis=-1, keepdims=True)
    gsum = jnp.sum(jnp.where(gmask, jnp.exp(lg - gmax), 0.0), axis=-1, keepdims=True)
    g_w = 1.0 / gsum
    g_idx = first(gmask & (lg == gmax))
    lo = N_GROUPS + EXPERTS_PER_GROUP * g_idx
    emask = (lane >= lo) & (lane < lo + EXPERTS_PER_GROUP)
    e1 = jnp.max(jnp.where(emask, lg, ninf), axis=-1, keepdims=True)
    i1 = first(emask & (lg == e1))
    rest = emask & (lane != i1)
    e2 = jnp.max(jnp.where(rest, lg, ninf), axis=-1, keepdims=True)
    i2 = first(rest & (lg == e2))
    tt = jnp.exp(e2 - e1)
    w1 = g_w / (1.0 + tt)
    comb_ref[...] = (jnp.where(lane == i1, w1, 0.0) + jnp.where(lane == i2, w1 * tt, 0.0)
                     + jnp.where(lane == 0, g_idx.astype(F32), 0.0))


def _outproj_ln_router(att, hyo, x, wa, wh, bo, g, b, wr_hi, wr_lo, br, *, alpha):
    B, L, D = x.shape
    tm = PROJ_TILE
    tok = lambda w: pl.BlockSpec((None, tm, w), lambda bb, i: (bb, i, 0))
    return pl.pallas_call(
        functools.partial(_outproj_kernel, alpha=alpha),
        grid=(B, L // tm),
        in_specs=[tok(att.shape[2]), tok(hyo.shape[2]), tok(D)]
        + [_full(a) for a in (wa, wh, bo, g, b, wr_hi, wr_lo, br)],
        out_specs=(tok(D), tok(ROUTER_LANES)),
        out_shape=(jax.ShapeDtypeStruct((B, L, D), F32), jax.ShapeDtypeStruct((B, L, ROUTER_LANES), F32)),
        compiler_params=_params("parallel", "parallel"),
        name="outproj_ln_router",
    )(att, hyo, x, wa, wh, bo, g, b, wr_hi, wr_lo, br)


def _moe_kernel(meta_ref, x_ref, comb_ref, w1_ref, w3_ref, w2_ref, g_ref, b_ref, o_ref,
                xs_ref, ys_ref, cs_ref, pt_ref, *, alpha, tb, slots, tile):
    bi = pl.program_id(0)
    blk = pl.program_id(1)
    step = pl.program_id(2)
    grp = step // (EXPERTS_PER_GROUP // MOE_EXPERTS_PER_STEP)

    @pl.when(step == 0)
    def _():
        comb = comb_ref[...]
        starts = [meta_ref[bi, blk, g].astype(F32) for g in range(N_GROUPS)]
        gid_row = comb.T[0:1, :]
        grow = lax.broadcasted_iota(jnp.int32, (8, tb), 0).astype(F32)
        oh_row = jnp.where(gid_row == grow, 1.0, 0.0)
        r_i = lax.broadcasted_iota(jnp.int32, (tb, tb), 0)
        c_i = lax.broadcasted_iota(jnp.int32, (tb, tb), 1)
        earlier_row = jnp.where(r_i < c_i, 1.0, 0.0).astype(BF16)
        rank_row = jnp.sum(oh_row * _dot(oh_row.astype(BF16), earlier_row), axis=0, keepdims=True)
        pos_row = rank_row
        for g in range(N_GROUPS):
            pos_row = pos_row + oh_row[g:g + 1] * starts[g]
        slot_r = lax.broadcasted_iota(jnp.int32, (slots, tb), 0).astype(F32)
        p = jnp.where(slot_r == pos_row, 1.0, 0.0).astype(BF16)
        ch, cl = _split_bf16(comb)
        d = x_ref.shape[1]
        moved = _dot(p, jnp.concatenate([x_ref[...].astype(BF16), ch, cl], axis=1))
        xs_ref[...] = moved[:, :d].astype(BF16)
        cs_ref[...] = moved[:, d:d + ROUTER_LANES] + moved[:, d + ROUTER_LANES:]

        gid_col = comb[:, 0:1]
        glane = lax.broadcasted_iota(jnp.int32, comb.shape, 1).astype(F32)
        oh_col = jnp.where(gid_col == glane, 1.0, 0.0)
        earlier_col = jnp.where(c_i < r_i, 1.0, 0.0).astype(BF16)
        rank_col = jnp.sum(oh_col * _dot(earlier_col, oh_col.astype(BF16)), axis=1, keepdims=True)
        pos_col = rank_col
        for g in range(N_GROUPS):
            pos_col = pos_col + oh_col[:, g:g + 1] * starts[g]
        slot_c = lax.broadcasted_iota(jnp.int32, (tb, slots), 1).astype(F32)
        pt_ref[...] = jnp.where(slot_c == pos_col, 1.0, 0.0).astype(BF16)
        ys_ref[...] = jnp.zeros_like(ys_ref)

    start = meta_ref[bi, blk, grp]
    n_tiles = meta_ref[bi, blk, N_GROUPS + grp]

    def row_tile(i, carry):
        rows = pl.ds(pl.multiple_of(start + i * tile, MOE_GROUP_ALIGN), tile)
        xt = xs_ref[rows, :]
        c = cs_ref[rows, :]
        lane = lax.broadcasted_iota(jnp.int32, c.shape, 1)
        hid = [jax.nn.silu(_dot(xt, w1_ref[k])) * _dot(xt, w3_ref[k]) for k in range(MOE_EXPERTS_PER_STEP)]
        y = None
        for k in range(MOE_EXPERTS_PER_STEP):
            e = N_GROUPS + step * MOE_EXPERTS_PER_STEP + k
            ce = jnp.sum(jnp.where(lane == e, c, 0.0), axis=-1, keepdims=True)
            yk = ce * _dot(hid[k].astype(BF16), w2_ref[k])
            y = yk if y is None else y + yk
        ys_ref[rows, :] += y
        return carry

    lax.fori_loop(0, n_tiles, row_tile, 0)

    @pl.when(step == pl.num_programs(2) - 1)
    def _():
        f = _dot(pt_ref[...], ys_ref[...].astype(BF16))
        o_ref[...] = _layer_norm(alpha * x_ref[...] + f, g_ref[...], b_ref[...])


def _moe_ln(x1, comb, w1, w3, w2, g, b, *, alpha):
    B, L, D = x1.shape
    E, _, de = w1.shape
    tb = min(MOE_BLOCK, L)
    tile = MOE_ROW_TILE
    per = MOE_EXPERTS_PER_STEP
    nblk = L // tb
    slots = tb + 2 * tile
    gid = comb[..., 0].astype(jnp.int32).reshape(B, nblk, tb)
    cnt = jnp.sum(gid[..., None] == jnp.arange(N_GROUPS), axis=2).astype(jnp.int32)
    padded = (cnt + MOE_GROUP_ALIGN - 1) // MOE_GROUP_ALIGN * MOE_GROUP_ALIGN
    meta = jnp.concatenate([jnp.cumsum(padded, axis=-1) - padded, (cnt + tile - 1) // tile], axis=-1)

    tok = lambda w: pl.BlockSpec((None, tb, w), lambda bb, i, e, m: (bb, i, 0))
    const = lambda a: pl.BlockSpec(a.shape, lambda bb, i, e, m: (0,) * a.ndim)
    return pl.pallas_call(
        functools.partial(_moe_kernel, alpha=alpha, tb=tb, slots=slots, tile=tile),
        grid_spec=pltpu.PrefetchScalarGridSpec(
            num_scalar_prefetch=1,
            grid=(B, nblk, E // per),
            in_specs=[tok(D), tok(ROUTER_LANES),
                      pl.BlockSpec((per, D, de), lambda bb, i, s, m: (s, 0, 0)),
                      pl.BlockSpec((per, D, de), lambda bb, i, s, m: (s, 0, 0)),
                      pl.BlockSpec((per, de, D), lambda bb, i, s, m: (s, 0, 0)),
                      const(g), const(b)],
            out_specs=tok(D),
            scratch_shapes=[pltpu.VMEM((slots, D), BF16), pltpu.VMEM((slots, D), F32),
                            pltpu.VMEM((slots, ROUTER_LANES), F32), pltpu.VMEM((tb, slots), BF16)]),
        out_shape=jax.ShapeDtypeStruct((B, L, D), F32),
        compiler_params=_params("parallel", "parallel", "arbitrary"),
        name="moe_ln",
    )(meta, x1, comb, w1, w3, w2, g, b)


def _alibi_tables():
    t = ATT_TILE
    slopes = jnp.asarray(LOG2E * 2.0 ** (-8.0 * np.arange(1, ATT_HEADS + 1) / ATT_HEADS), dtype=F32)
    off = (jnp.arange(t)[:, None] - jnp.arange(t)[None, :]).astype(F32)
    dpos = slopes[:, None, None] * off[None]
    return slopes, jnp.stack([dpos, -jnp.abs(dpos), -dpos], axis=1)


def _prep_layer(l, depth, w_in, b_in, conv_w, conv_b, lam_q1, lam_k1, lam_q2, lam_k2, subln_g,
                filt_w1, filt_b1, filt_freq1, filt_w2, filt_b2, filt_freq2, filt_w3, hyena_d,
                w_out, b_out, ln1_g, ln1_b, router_group_w, router_group_b, router_expert_w,
                router_expert_b, exp_w1, exp_w3, exp_w2, ln2_g, ln2_b):
    a0, a1, a2 = ATT_QK_WIDTH, 2 * ATT_QK_WIDTH, 2 * ATT_QK_WIDTH + ATT_WIDTH
    scale = ATT_HEAD_DIM ** -0.5 * LOG2E
    row = lambda v: v.reshape(1, -1)
    w, b = w_in[l], b_in[l]
    c = hyena_d.shape[1]
    wr = jnp.concatenate([router_group_w[l], router_expert_w[l]], axis=1)
    wr = jnp.pad(wr, ((0, 0), (0, ROUTER_LANES - wr.shape[1])))
    wr_hi = wr.astype(BF16)
    br = jnp.concatenate([router_group_b[l], router_expert_b[l]])
    max_decay = abs(math.log(DECAY_TARGET) / FAST_DECAY_PCT)
    min_decay = abs(math.log(DECAY_TARGET) / SLOW_DECAY_PCT)
    return dict(
        lam_init=0.8 - 0.6 * math.exp(-0.3 * l),
        alpha=(2.0 * depth) ** 0.25,
        wq=(w[:, :a0] * scale).astype(BF16), bq=row(b[:a0] * scale),
        wk=w[:, a0:a1].astype(BF16), bk=row(b[a0:a1]),
        wvt=w[:, a1:a2].T.astype(BF16), bvt=b[a1:a2].reshape(-1, 1),
        why=w[:, a2:].astype(BF16), bhy=row(b[a2:]),
        conv_w=conv_w[l], conv_b=row(conv_b[l]),
        lamv=jnp.stack([lam_q1[l], lam_k1[l], lam_q2[l], lam_k2[l]]).astype(F32),
        subln_g=row(subln_g[l]),
        filt_w1=jnp.pad(filt_w1[l], ((0, LANES - FILTER_EMB), (0, 0))), filt_b1=row(filt_b1[l]),
        filt_f1=row(filt_freq1[l]), filt_w2=filt_w2[l], filt_b2=row(filt_b2[l]), filt_f2=row(filt_freq2[l]),
        filt_w3=filt_w3[l],
        deltas=row(jnp.linspace(min_decay, max_decay, c, dtype=F32)),
        hyena_d=row(hyena_d[l]),
        wa=w_out[l][:ATT_WIDTH].astype(BF16), wh=w_out[l][ATT_WIDTH:].astype(BF16), bo=row(b_out[l]),
        ln1_g=row(ln1_g[l]), ln1_b=row(ln1_b[l]),
        wr_hi=wr_hi, wr_lo=(wr - wr_hi.astype(F32)).astype(BF16),
        br=row(jnp.pad(br, (0, ROUTER_LANES - br.shape[0]))),
        w1=exp_w1[l].astype(BF16), w3=exp_w3[l].astype(BF16), w2=exp_w2[l].astype(BF16),
        ln2_g=row(ln2_g[l]), ln2_b=row(ln2_b[l]),
    )


def _layer(x, lp, alibi):
    slopes, dist = alibi
    q, k, vt, phy = _inproj(x, lp["wq"], lp["bq"], lp["wk"], lp["bk"], lp["wvt"], lp["bvt"],
                            lp["why"], lp["bhy"])
    att = _attention(q, k, vt, slopes, dist, lp["lamv"], lp["subln_g"],
                     lam_init=lp["lam_init"])
    hyo = _hyena(phy, lp)
    x1, comb = _outproj_ln_router(att, hyo, x, lp["wa"], lp["wh"], lp["bo"], lp["ln1_g"], lp["ln1_b"],
                                  lp["wr_hi"], lp["wr_lo"], lp["br"], alpha=lp["alpha"])
    return _moe_ln(x1, comb, lp["w1"], lp["w3"], lp["w2"], lp["ln2_g"], lp["ln2_b"], alpha=lp["alpha"])


def kernel(x_prompt, x_sample, w_in, b_in, conv_w, conv_b, lam_q1, lam_k1, lam_q2, lam_k2, subln_g, filt_w1, filt_b1, filt_freq1, filt_w2, filt_b2, filt_freq2, filt_w3, hyena_d, w_out, b_out, ln1_g, ln1_b, router_group_w, router_group_b, router_expert_w, router_expert_b, exp_w1, exp_w3, exp_w2, ln2_g, ln2_b):
    params = (w_in, b_in, conv_w, conv_b, lam_q1, lam_k1, lam_q2, lam_k2, subln_g, filt_w1, filt_b1,
              filt_freq1, filt_w2, filt_b2, filt_freq2, filt_w3, hyena_d, w_out, b_out, ln1_g, ln1_b,
              router_group_w, router_group_b, router_expert_w, router_expert_b, exp_w1, exp_w3, exp_w2,
              ln2_g, ln2_b)
    depth = w_in.shape[0]
    layers = [_prep_layer(l, depth, *params) for l in range(depth)]
    alibi = _alibi_tables()

    def trunk(x):
        for lp in layers:
            x = _layer(x, lp, alibi)
        return x

    return (trunk(x_prompt), trunk(x_sample))
```
